```python
import math, functools
import jax, jax.numpy as jnp
from jax import lax
import numpy as np

D_MODEL = 1024
BATCH = 2
SEQ = 8192
DEPTH = 2
DEC_BATCH = 128
DEC_SEQ = 1
PAST_LEN = 8192
PAGE_SIZE = 128

BRANCH_W = D_MODEL // 2
HEAD_DIM = 64
N_HEADS = BRANCH_W // HEAD_DIM
KV_HEADS = 2
GQA_GROUP = N_HEADS // KV_HEADS
WINDOW = 128
ATTN_BLOCK = WINDOW
Q_W = N_HEADS * HEAD_DIM
KV_W = KV_HEADS * HEAD_DIM
SSM_GROUP_CH = 16
SSM_GROUPS = BRANCH_W // SSM_GROUP_CH
SSM_STATE = 64
CHUNK = 128
SGU_GROUPS = 4
N_BRANCH = 3
IN_SIZES = (Q_W, KV_W, KV_W, BRANCH_W, BRANCH_W, BRANCH_W, BRANCH_W, BRANCH_W, BRANCH_W, N_BRANCH * D_MODEL)
N_IN = sum(IN_SIZES)
ALPHA = (2 * DEPTH) ** 0.25
BETA = (8 * DEPTH) ** -0.25
LN_EPS = 1e-5
ATTN_SCALE = HEAD_DIM ** -0.5
NEG_INF = -1e30

kernel_name = 'hybrid_swa_s5_sgu_decoder_step'


def _layernorm(x, g, b):
    xf = x.astype(jnp.float32)
    xc = xf - xf.mean(-1, keepdims=True)
    var = (xc * xc).mean(-1, keepdims=True)
    y = xc * lax.rsqrt(var + LN_EPS) * g.astype(jnp.float32) + b.astype(jnp.float32)
    return y.astype(x.dtype)


def _alibi_slopes():
    h = jnp.arange(1, N_HEADS + 1, dtype=jnp.float32)
    return jnp.exp2(-8.0 * h / N_HEADS)


def _window_attention(q, k, v, qpos, kpos, sinks):
    nb, m, tq = q.shape[:3]
    qg = q.reshape(nb, m, tq, KV_HEADS, GQA_GROUP, HEAD_DIM)
    s = jnp.einsum('bmqkgd,bmskd->bmkgqs', qg, k, preferred_element_type=jnp.float32) * ATTN_SCALE
    dist = qpos[:, :, None] - kpos[:, None, :]
    valid = (dist >= 0) & (dist <= WINDOW) & (kpos[:, None, :] >= 0)
    slopes = _alibi_slopes().reshape(KV_HEADS, GQA_GROUP)
    bias = -slopes[None, None, :, :, None, None] * dist.astype(jnp.float32)[None, :, None, None]
    s = jnp.where(valid[None, :, None, None], s + bias, NEG_INF)
    sink = sinks.astype(jnp.float32).reshape(KV_HEADS, GQA_GROUP)[None, None, :, :, None, None]
    mx = jnp.maximum(s.max(-1, keepdims=True), sink)
    p = jnp.exp(s - mx)
    denom = p.sum(-1, keepdims=True) + jnp.exp(sink - mx)
    p = (p / denom).astype(v.dtype)
    o = jnp.einsum('bmkgqs,bmskd->bmqkgd', p, v)
    return o.reshape(nb, m, tq, Q_W)


def _attend_prompt(q, k, v, sinks):
    n, t = q.shape[:2]
    nb = t // ATTN_BLOCK
    qb = q.reshape(n, nb, ATTN_BLOCK, N_HEADS, HEAD_DIM)
    kb = k.reshape(n, nb, ATTN_BLOCK, KV_HEADS, HEAD_DIM)
    vb = v.reshape(n, nb, ATTN_BLOCK, KV_HEADS, HEAD_DIM)
    shift = ((0, 0), (1, 0), (0, 0), (0, 0), (0, 0))
    kcat = jnp.concatenate([jnp.pad(kb[:, :-1], shift), kb], axis=2)
    vcat = jnp.concatenate([jnp.pad(vb[:, :-1], shift), vb], axis=2)
    pos = jnp.arange(t, dtype=jnp.int32).reshape(nb, ATTN_BLOCK)
    kpos = jnp.concatenate([pos - ATTN_BLOCK, pos], axis=1)
    o = _window_attention(qb, kcat, vcat, pos, kpos, sinks).reshape(n, t, Q_W)
    keep = min(WINDOW, t)
    return o, k[:, t - keep:], v[:, t - keep:]


def _attend_sample(q, k, v, sinks, cache_k, cache_v):
    t = q.shape[1]
    w = cache_k.shape[1]
    kcat = jnp.concatenate([cache_k.astype(k.dtype), k], axis=1)[:, None]
    vcat = jnp.concatenate([cache_v.astype(v.dtype), v], axis=1)[:, None]
    qpos = PAST_LEN + jnp.arange(t, dtype=jnp.int32)
    kpos = jnp.concatenate([PAST_LEN - w + jnp.arange(w, dtype=jnp.int32), qpos])
    o = _window_attention(q[:, None], kcat, vcat, qpos[None], kpos[None], sinks)[:, 0]
    return o, k, v


def _complex_affine_combine(e1, e2):
    a1r, a1i, b1r, b1i = e1
    a2r, a2i, b2r, b2i = e2
    return (a2r * a1r - a2i * a1i,
            a2r * a1i + a2i * a1r,
            a2r * b1r - a2i * b1i + b2r,
            a2r * b1i + a2i * b1r + b2i)


def _s5(u, h0_re, h0_im, lam_re, lam_im, log_dt, b_re, b_im, c_re, c_im, d_skip, glu_w, glu_b):
    f32 = jnp.float32
    n, t, _ = u.shape
    uf = u.astype(f32)
    ug = uf.reshape(n, t, SSM_GROUPS, SSM_GROUP_CH)
    lr = lam_re.astype(f32)
    li = lam_im.astype(f32)
    dt = jnp.exp(log_dt.astype(f32))[:, None]
    mag = jnp.exp(lr * dt)
    ar = mag * jnp.cos(li * dt)
    ai = mag * jnp.sin(li * dt)
    den = lr * lr + li * li
    cr = ((ar - 1.0) * lr + ai * li) / den
    ci = (ai * lr - (ar - 1.0) * li) / den
    br = b_re.astype(f32)
    bi = b_im.astype(f32)
    bbr = cr[..., None] * br - ci[..., None] * bi
    bbi = cr[..., None] * bi + ci[..., None] * br
    xr = jnp.einsum('ntgh,gph->ntgp', ug, bbr)
    xi = jnp.einsum('ntgh,gph->ntgp', ug, bbi)
    h0r = h0_re.astype(f32)
    h0i = h0_im.astype(f32)
    xr = xr.at[:, 0].add(ar * h0r - ai * h0i)
    xi = xi.at[:, 0].add(ar * h0i + ai * h0r)
    a_r = jnp.broadcast_to(ar, xr.shape)
    a_i = jnp.broadcast_to(ai, xi.shape)
    _, _, hr, hi = lax.associative_scan(_complex_affine_combine, (a_r, a_i, xr, xi), axis=1)
    y = (jnp.einsum('ntgp,ghp->ntgh', hr, c_re.astype(f32))
         - jnp.einsum('ntgp,ghp->ntgh', hi, c_im.astype(f32)))
    y = y.reshape(n, t, BRANCH_W) + d_skip.astype(f32) * uf
    y = jax.nn.gelu(y)
    y = y * jax.nn.sigmoid(jnp.einsum('ntw,wv->ntv', y, glu_w.astype(f32)) + glu_b.astype(f32))
    return y.astype(u.dtype), hr[:, -1], hi[:, -1]


def _chunk_sgu(u, v, ln_g, ln_b, ws, bs):
    n, t, w = v.shape
    vn = _layernorm(v, ln_g, ln_b)
    pad = (-t) % CHUNK
    vp = jnp.pad(vn, ((0, 0), (0, pad), (0, 0)))
    nc = (t + pad) // CHUNK
    vc = vp.reshape(n, nc, CHUNK, SGU_GROUPS, w // SGU_GROUPS)
    tril = jnp.tril(jnp.ones((CHUNK, CHUNK), dtype=bool))
    wm = jnp.where(tril[None], ws, jnp.zeros_like(ws))
    s = jnp.einsum('gts,ncsgw->nctgw', wm, vc) + bs.T[None, None, :, :, None]
    s = s.reshape(n, nc * CHUNK, w)[:, :t]
    return u * s, vn


def _layer(x, attend, h0_re, h0_im, w_in, sinks, lam_re, lam_im, log_dt, b_re, b_im, c_re, c_im,
           d_skip, glu_w, glu_b, sgu_ln_g, sgu_ln_b, sgu_w, sgu_b, w_read, w_o, ln_g, ln_b):
    n, t, _ = x.shape
    proj = jnp.einsum('ntd,de->nte', x, w_in)
    cuts = [int(c) for c in np.cumsum(IN_SIZES)[:-1]]
    q, k, v, z_a, u_b, z_b, u_c, v_c, z_c, g = jnp.split(proj, cuts, axis=-1)
    q = q.reshape(n, t, N_HEADS, HEAD_DIM)
    k = k.reshape(n, t, KV_HEADS, HEAD_DIM)
    v = v.reshape(n, t, KV_HEADS, HEAD_DIM)
    y_a, k_rows, v_rows = attend(q, k, v, sinks)
    y_b, h_re, h_im = _s5(u_b, h0_re, h0_im, lam_re, lam_im, log_dt, b_re, b_im, c_re, c_im,
                          d_skip, glu_w, glu_b)
    y_c, vn_rows = _chunk_sgu(u_c, v_c, sgu_ln_g, sgu_ln_b, sgu_w, sgu_b)
    ys = jnp.stack([y_a * jax.nn.silu(z_a), y_b * jax.nn.silu(z_b), y_c * jax.nn.silu(z_c)], axis=2)
    branch = jnp.einsum('ntbw,bwd->ntbd', ys, w_read)
    gates = jax.nn.sigmoid(g.reshape(n, t, N_BRANCH, D_MODEL))
    merged = (gates * branch).sum(axis=2)
    out = jnp.einsum('ntd,de->nte', merged, w_o)
    x = _layernorm(ALPHA * x + out, ln_g, ln_b)
    return x, (k_rows, v_rows, h_re, h_im, vn_rows)


def setup_inputs(seed: int = 0) -> dict:
    key = jax.random.key(seed)
    ks = jax.random.split(key, 26)
    f32 = jnp.float32

    def nrm(k, shape, scale):
        return scale * jax.random.normal(k, shape, f32)

    win_buf = min(WINDOW, PAST_LEN)
    lam_im_base = jnp.pi * jnp.arange(SSM_STATE, dtype=f32)
    return {
        'x_prompt': nrm(ks[0], (BATCH, SEQ, D_MODEL), 1.0),
        'x_sample': nrm(ks[1], (DEC_BATCH, DEC_SEQ, D_MODEL), 1.0),
        'cache_k_win': nrm(ks[2], (DEPTH, DEC_BATCH, win_buf, KV_HEADS, HEAD_DIM), 1.0),
        'cache_v_win': nrm(ks[3], (DEPTH, DEC_BATCH, win_buf, KV_HEADS, HEAD_DIM), 1.0),
        'state_ssm_re': nrm(ks[4], (DEPTH, DEC_BATCH, SSM_GROUPS, SSM_STATE), 0.3),
        'state_ssm_im': nrm(ks[5], (DEPTH, DEC_BATCH, SSM_GROUPS, SSM_STATE), 0.3),
        'w_in': nrm(ks[6], (DEPTH, D_MODEL, N_IN), D_MODEL ** -0.5),
        'attn_sinks': nrm(ks[7], (DEPTH, N_HEADS), 0.5),
        'ssm_lambda_re': -0.5 + nrm(ks[8], (DEPTH, SSM_GROUPS, SSM_STATE), 0.01),
        'ssm_lambda_im': lam_im_base + nrm(ks[9], (DEPTH, SSM_GROUPS, SSM_STATE), 0.01),
        'ssm_log_dt': jax.random.uniform(ks[10], (DEPTH, SSM_GROUPS), f32,
                                         minval=math.log(1e-3), maxval=math.log(1e-1)),
        'ssm_b_re': nrm(ks[11], (DEPTH, SSM_GROUPS, SSM_STATE, SSM_GROUP_CH), (2 * SSM_GROUP_CH) ** -0.5),
        'ssm_b_im': nrm(ks[12], (DEPTH, SSM_GROUPS, SSM_STATE, SSM_GROUP_CH), (2 * SSM_GROUP_CH) ** -0.5),
        'ssm_c_re': nrm(ks[13], (DEPTH, SSM_GROUPS, SSM_GROUP_CH, SSM_STATE), SSM_STATE ** -0.5),
        'ssm_c_im': nrm(ks[14], (DEPTH, SSM_GROUPS, SSM_GROUP_CH, SSM_STATE), SSM_STATE ** -0.5),
        'ssm_d': nrm(ks[15], (DEPTH, BRANCH_W), 1.0),
        'glu_w': nrm(ks[16], (DEPTH, BRANCH_W, BRANCH_W), BRANCH_W ** -0.5),
        'glu_b': nrm(ks[17], (DEPTH, BRANCH_W), 0.02),
        'sgu_ln_g': 1.0 + nrm(ks[18], (DEPTH, BRANCH_W), 0.02),
        'sgu_ln_b': nrm(ks[19], (DEPTH, BRANCH_W), 0.02),
        'sgu_w': nrm(ks[20], (DEPTH, SGU_GROUPS, CHUNK, CHUNK), 0.5 * CHUNK ** -0.5),
        'sgu_b': 1.0 + nrm(ks[21], (DEPTH, SGU_GROUPS, CHUNK), 0.02),
        'w_read': nrm(ks[22], (DEPTH, N_BRANCH, BRANCH_W, D_MODEL), BRANCH_W ** -0.5),
        'w_o': nrm(ks[23], (DEPTH, D_MODEL, D_MODEL), BETA * D_MODEL ** -0.5),
        'ln_g': 1.0 + nrm(ks[24], (DEPTH, D_MODEL), 0.02),
        'ln_b': nrm(ks[25], (DEPTH, D_MODEL), 0.02),
    }


def reference(x_prompt, x_sample, cache_k_win, cache_v_win, state_ssm_re, state_ssm_im,
              w_in, attn_sinks, ssm_lambda_re, ssm_lambda_im, ssm_log_dt, ssm_b_re, ssm_b_im,
              ssm_c_re, ssm_c_im, ssm_d, glu_w, glu_b, sgu_ln_g, sgu_ln_b, sgu_w, sgu_b,
              w_read, w_o, ln_g, ln_b):
    xp = x_prompt
    xs = x_sample
    h0 = jnp.zeros((x_prompt.shape[0], SSM_GROUPS, SSM_STATE), jnp.float32)
    kp, vp, hrp, hip = [], [], [], []
    ksm, vsm, hrs, his, vcs = [], [], [], [], []
    for l in range(DEPTH):
        lw = (w_in[l], attn_sinks[l], ssm_lambda_re[l], ssm_lambda_im[l], ssm_log_dt[l],
              ssm_b_re[l], ssm_b_im[l], ssm_c_re[l], ssm_c_im[l], ssm_d[l], glu_w[l], glu_b[l],
              sgu_ln_g[l], sgu_ln_b[l], sgu_w[l], sgu_b[l], w_read[l], w_o[l], ln_g[l], ln_b[l])
        xp, (k_r, v_r, h_r, h_i, _) = _layer(xp, _attend_prompt, h0, h0, *lw)
        kp.append(k_r)
        vp.append(v_r)
        hrp.append(h_r)
        hip.append(h_i)
        attend_s = functools.partial(_attend_sample, cache_k=cache_k_win[l], cache_v=cache_v_win[l])
        xs, (k_r, v_r, h_r, h_i, vc) = _layer(xs, attend_s, state_ssm_re[l], state_ssm_im[l], *lw)
        ksm.append(k_r)
        vsm.append(v_r)
        hrs.append(h_r)
        his.append(h_i)
        vcs.append(vc)
    return (xp, xs, jnp.stack(kp), jnp.stack(vp), jnp.stack(hrp), jnp.stack(hip),
            jnp.stack(ksm), jnp.stack(vsm), jnp.stack(hrs), jnp.stack(his), jnp.stack(vcs))
```

```python
import functools
import math

import jax
import jax.numpy as jnp
from jax import lax
from jax.experimental import pallas as pl
from jax.experimental.pallas import tpu as pltpu

D_MODEL = 1024
BRANCH_W = 512
HEAD_DIM = 64
N_HEADS = 8
KV_HEADS = 2
WINDOW = 128
SSM_GROUP_CH = 16
SSM_GROUPS = 32
SSM_STATE = 64
SSM_W = SSM_GROUPS * SSM_STATE
CHUNK = 128
SGU_GROUPS = 4
N_BRANCH = 3
DEPTH = 2
ALPHA = (2 * DEPTH) ** 0.25
LN_EPS = 1e-5
ATTN_SCALE = HEAD_DIM ** -0.5
NEG_INF = -1e30

Q0, K0, V0, ZA0, UB0, ZB0, UC0, VC0, ZC0, G0 = 0, 512, 640, 768, 1280, 1792, 2304, 2816, 3328, 3840
N_IN = 6912

LANES = 128
SUBLANES = 8
VMEM_LIMIT_BYTES = 60 * 1024 * 1024

TB = 128
SCAN_STEPS = int(math.log2(TB))
SCAN_PAD = TB // 2
SCAN_CT = 512
PROJ_CT = 768

HEAD_PERM = (0, 4, 1, 5, 2, 6, 3, 7)

F32 = jnp.float32
BF16 = jnp.bfloat16


def _sigmoid(x):
    return 1.0 / (1.0 + jnp.exp(-x))


def _silu(x):
    return x * _sigmoid(x)


def _gelu_tanh(x):
    c = math.sqrt(2.0 / math.pi)
    return 0.5 * x * (1.0 + jnp.tanh(c * (x + 0.044715 * (x * x * x))))


def _layernorm(x, g, b):
    mu = jnp.mean(x, axis=-1, keepdims=True)
    xc = x - mu
    var = jnp.mean(xc * xc, axis=-1, keepdims=True)
    return xc * lax.rsqrt(var + LN_EPS) * g + b


def _dot(a, b):
    return jnp.dot(a, b, preferred_element_type=F32)


def _dot_nt(a, b):
    return lax.dot_general(a, b, (((1,), (1,)), ((), ())), preferred_element_type=F32)


def _slope(h):
    return 2.0 ** (-(h + 1.0))


def _ssm_prep_kernel(lr_ref, li_ref, ldt_ref, brt_ref, bit_ref, ar_ref, ai_ref, bbr_ref, bbi_ref):
    lr = lr_ref[...]
    li = li_ref[...]
    dt = jnp.exp(ldt_ref[...])
    mag = jnp.exp(lr * dt)
    ar = mag * jnp.cos(li * dt)
    ai = mag * jnp.sin(li * dt)
    den = lr * lr + li * li
    cr = ((ar - 1.0) * lr + ai * li) / den
    ci = (ai * lr - (ar - 1.0) * li) / den
    ar_ref[...] = ar
    ai_ref[...] = ai
    for h in range(SSM_GROUP_CH):
        br = brt_ref[h]
        bi = bit_ref[h]
        bbr_ref[h] = cr * br - ci * bi
        bbi_ref[h] = cr * bi + ci * br


def _ssm_prep(lam_re, lam_im, log_dt, b_re, b_im):
    brt = jnp.transpose(b_re, (2, 0, 1))
    bit = jnp.transpose(b_im, (2, 0, 1))
    gp = jax.ShapeDtypeStruct((SSM_GROUPS, SSM_STATE), F32)
    hgp = jax.ShapeDtypeStruct((SSM_GROUP_CH, SSM_GROUPS, SSM_STATE), F32)
    return pl.pallas_call(
        _ssm_prep_kernel,
        out_shape=(gp, gp, hgp, hgp),
        name="ssm_prep",
    )(lam_re, lam_im, log_dt.reshape(SSM_GROUPS, 1), brt, bit)


def _ssm_matrices(ar, ai, bbr, bbi, c_re, c_im):
    eye = jnp.eye(SSM_GROUPS, dtype=F32)
    bb_r = jnp.einsum('hgp,gk->ghkp', bbr, eye).reshape(BRANCH_W, SSM_W)
    bb_i = jnp.einsum('hgp,gk->ghkp', bbi, eye).reshape(BRANCH_W, SSM_W)
    bb = jnp.concatenate([bb_r, bb_i], axis=1).astype(BF16)
    cc_r = jnp.einsum('ghp,gk->gpkh', c_re, eye).reshape(SSM_W, BRANCH_W)
    cc_i = jnp.einsum('ghp,gk->gpkh', c_im, eye).reshape(SSM_W, BRANCH_W)
    cc = jnp.concatenate([cc_r, -cc_i], axis=0).astype(BF16)
    a_row = jnp.concatenate([ar.reshape(1, SSM_W), ai.reshape(1, SSM_W)], axis=1)
    return a_row, bb, cc


def _project(x, w_in_ref, proj_ref):
    xb = x.astype(BF16)
    for c0 in range(0, N_IN, PROJ_CT):
        proj_ref[:, c0:c0 + PROJ_CT] = _dot(xb, w_in_ref[:, c0:c0 + PROJ_CT])


def _s5_output(h_bf, u, cc_ref, dskip_ref, gluw_ref, glub_ref):
    y = _dot(h_bf, cc_ref[...]) + dskip_ref[...] * u
    y = _gelu_tanh(y)
    return y * _sigmoid(_dot(y.astype(BF16), gluw_ref[...]) + glub_ref[...])


def _merge(x, proj_ref, ybr_ref, wread_ref, wo_ref, lng_ref, lnb_ref):
    merged = None
    for b, z0 in enumerate((ZA0, ZB0, ZC0)):
        yb = ybr_ref[:, b * BRANCH_W:(b + 1) * BRANCH_W] * _silu(proj_ref[:, z0:z0 + BRANCH_W])
        branch = _dot(yb.astype(BF16), wread_ref[b])
        gate = _sigmoid(proj_ref[:, G0 + b * D_MODEL:G0 + (b + 1) * D_MODEL])
        term = gate * branch
        merged = term if merged is None else merged + term
    out = _dot(merged.astype(BF16), wo_ref[...])
    return _layernorm(ALPHA * x + out, lng_ref[...], lnb_ref[...])


def _prompt_kernel(sinks_ref, x_ref, w_in_ref, arow_ref, bb_ref, cc_ref, dskip_ref, gluw_ref, glub_ref,
                   sglng_ref, sglnb_ref, sguw_ref, sgubt_ref, wread_ref, wo_ref, lng_ref, lnb_ref,
                   y_ref, kvwin_ref, hstate_ref,
                   proj_ref, ybr_ref, kvprev_ref, bias_ref, apow_ref, ha_ref, hb_ref, hcarry_ref):
    bi = pl.program_id(0)
    ji = pl.program_id(1)

    @pl.when(jnp.logical_and(bi == 0, ji == 0))
    def _init_tables():
        qa = lax.broadcasted_iota(jnp.int32, (TB, 2 * TB), 0)
        kk = lax.broadcasted_iota(jnp.int32, (TB, 2 * TB), 1)
        dist = qa + WINDOW - kk
        valid = jnp.logical_and(dist >= 0, dist <= WINDOW)
        distf = dist.astype(F32)
        for h in range(N_HEADS):
            bias_ref[h] = jnp.where(valid, -_slope(h) * distf, NEG_INF)
        pr = arow_ref[:, 0:SSM_W]
        pi = arow_ref[:, SSM_W:2 * SSM_W]
        for k in range(SCAN_STEPS):
            apow_ref[k, :, 0:SSM_W] = jnp.broadcast_to(pr, (SUBLANES, SSM_W))
            apow_ref[k, :, SSM_W:2 * SSM_W] = jnp.broadcast_to(pi, (SUBLANES, SSM_W))
            pr, pi = pr * pr - pi * pi, 2.0 * (pr * pi)
        ha_ref[0:SCAN_PAD, :] = jnp.zeros((SCAN_PAD, 2 * SSM_W), F32)
        hb_ref[0:SCAN_PAD, :] = jnp.zeros((SCAN_PAD, 2 * SSM_W), F32)

    @pl.when(ji == 0)
    def _reset_carries():
        kvprev_ref[...] = jnp.zeros((TB, 2 * LANES), F32)
        hcarry_ref[...] = jnp.zeros((SUBLANES, 2 * SSM_W), F32)

    x = x_ref[0]
    _project(x, w_in_ref, proj_ref)

    kv_cur = proj_ref[:, K0:K0 + 2 * LANES]
    kv_prev = kvprev_ref[...]
    kcat = jnp.concatenate([kv_prev[:, 0:LANES], kv_cur[:, 0:LANES]], axis=0).astype(BF16)
    vcat = jnp.concatenate([kv_prev[:, LANES:], kv_cur[:, LANES:]], axis=0).astype(BF16)
    first_neg = jnp.where(ji == 0, NEG_INF, 0.0).astype(F32)
    kcol = lax.broadcasted_iota(jnp.int32, (TB, 2 * TB), 1)
    prev_mask = jnp.where(kcol < TB, first_neg, 0.0)
    lane = lax.broadcasted_iota(jnp.int32, (TB, LANES), 1)
    lo = lane < HEAD_DIM
    for t in range(N_HEADS // 2):
        qt = proj_ref[:, Q0 + t * LANES:Q0 + (t + 1) * LANES] * ATTN_SCALE
        halves = []
        for half, h in ((0, t), (1, t + N_HEADS // 2)):
            qm = jnp.where(lo if half == 0 else jnp.logical_not(lo), qt, 0.0).astype(BF16)
            s = _dot_nt(qm, kcat) + bias_ref[h] + prev_mask
            sink = sinks_ref[h]
            m = jnp.maximum(jnp.max(s, axis=-1, keepdims=True), sink)
            p = jnp.exp(s - m)
            denom = jnp.sum(p, axis=-1, keepdims=True) + jnp.exp(sink - m)
            o = _dot(p.astype(BF16), vcat)
            halves.append(o * (1.0 / denom))
        ybr_ref[:, t * LANES:(t + 1) * LANES] = jnp.where(lo, halves[0], halves[1])
    kvprev_ref[...] = kv_cur
    kvwin_ref[0] = kv_cur

    vn = _layernorm(proj_ref[:, VC0:VC0 + BRANCH_W], sglng_ref[...], sglnb_ref[...])
    trow = lax.broadcasted_iota(jnp.int32, (CHUNK, CHUNK), 0)
    tcol = lax.broadcasted_iota(jnp.int32, (CHUNK, CHUNK), 1)
    tril = trow >= tcol
    for g in range(SGU_GROUPS):
        wm = jnp.where(tril, sguw_ref[g], 0.0).astype(BF16)
        sg = _dot(wm, vn[:, g * LANES:(g + 1) * LANES].astype(BF16)) + sgubt_ref[:, g:g + 1]
        ybr_ref[:, 2 * BRANCH_W + g * LANES:2 * BRANCH_W + (g + 1) * LANES] = (
            proj_ref[:, UC0 + g * LANES:UC0 + (g + 1) * LANES] * sg)

    u = proj_ref[:, UB0:UB0 + BRANCH_W]
    ub = u.astype(BF16)
    for c0 in range(0, 2 * SSM_W, SCAN_CT):
        ha_ref[SCAN_PAD:SCAN_PAD + TB, c0:c0 + SCAN_CT] = _dot(ub, bb_ref[:, c0:c0 + SCAN_CT])
    hcr = hcarry_ref[SUBLANES - 1:SUBLANES, 0:SSM_W]
    hci = hcarry_ref[SUBLANES - 1:SUBLANES, SSM_W:2 * SSM_W]
    a_r = arow_ref[:, 0:SSM_W]
    a_i = arow_ref[:, SSM_W:2 * SSM_W]
    ha_ref[SCAN_PAD:SCAN_PAD + 1, 0:SSM_W] = ha_ref[SCAN_PAD:SCAN_PAD + 1, 0:SSM_W] + (a_r * hcr - a_i * hci)
    ha_ref[SCAN_PAD:SCAN_PAD + 1, SSM_W:2 * SSM_W] = (
        ha_ref[SCAN_PAD:SCAN_PAD + 1, SSM_W:2 * SSM_W] + (a_r * hci + a_i * hcr))

    srow = lax.broadcasted_iota(jnp.int32, (SUBLANES, SCAN_CT), 0)
    bufs = (ha_ref, hb_ref)
    for k in range(SCAN_STEPS):
        d = 1 << k
        src = bufs[k % 2]
        dst = bufs[(k + 1) % 2]
        for c0 in range(0, SSM_W, SCAN_CT):
            pw_r = apow_ref[k, :, c0:c0 + SCAN_CT]
            pw_i = apow_ref[k, :, SSM_W + c0:SSM_W + c0 + SCAN_CT]

            def tile_step(i, carry, d=d, src=src, dst=dst, c0=c0, pw_r=pw_r, pw_i=pw_i):
                r0 = pl.multiple_of(SCAN_PAD + i * SUBLANES, SUBLANES)
                cur_r = src[pl.ds(r0, SUBLANES), c0:c0 + SCAN_CT]
                cur_i = src[pl.ds(r0, SUBLANES), SSM_W + c0:SSM_W + c0 + SCAN_CT]
                if d >= SUBLANES:
                    rs = pl.multiple_of(SCAN_PAD + i * SUBLANES - d, SUBLANES)
                    sh_r = src[pl.ds(rs, SUBLANES), c0:c0 + SCAN_CT]
                    sh_i = src[pl.ds(rs, SUBLANES), SSM_W + c0:SSM_W + c0 + SCAN_CT]
                else:
                    rp = pl.multiple_of(SCAN_PAD + i * SUBLANES - SUBLANES, SUBLANES)
                    prev_r = src[pl.ds(rp, SUBLANES), c0:c0 + SCAN_CT]
                    prev_i = src[pl.ds(rp, SUBLANES), SSM_W + c0:SSM_W + c0 + SCAN_CT]
                    sh_r = jnp.where(srow >= d, pltpu.roll(cur_r, d, 0), pltpu.roll(prev_r, d, 0))
                    sh_i = jnp.where(srow >= d, pltpu.roll(cur_i, d, 0), pltpu.roll(prev_i, d, 0))
                dst[pl.ds(r0, SUBLANES), c0:c0 + SCAN_CT] = cur_r + (pw_r * sh_r - pw_i * sh_i)
                dst[pl.ds(r0, SUBLANES), SSM_W + c0:SSM_W + c0 + SCAN_CT] = cur_i + (pw_r * sh_i + pw_i * sh_r)
                return carry

            lax.fori_loop(0, TB // SUBLANES, tile_step, 0, unroll=4)
    hfin = bufs[SCAN_STEPS % 2]
    h_last = hfin[SCAN_PAD + TB - SUBLANES:SCAN_PAD + TB, :]
    hcarry_ref[...] = h_last
    hstate_ref[0] = h_last
    ybr_ref[:, BRANCH_W:2 * BRANCH_W] = _s5_output(
        hfin[SCAN_PAD:SCAN_PAD + TB, :].astype(BF16), u, cc_ref, dskip_ref, gluw_ref, glub_ref)

    y_ref[0] = _merge(x, proj_ref, ybr_ref, wread_ref, wo_ref, lng_ref, lnb_ref)


def _const_spec(shape):
    nd = len(shape)
    return pl.BlockSpec(shape, lambda b, j, _nd=nd: (0,) * _nd, pipeline_mode=pl.Buffered(1))


def _prompt_layer(x, sinks_p, lw):
    nb, t, _ = x.shape
    grid = (nb, t // TB)
    weights = (lw['w_in'], lw['a_row'], lw['bb'], lw['cc'], lw['d_skip'], lw['glu_w'], lw['glu_b'],
               lw['sgu_ln_g'], lw['sgu_ln_b'], lw['sgu_w'], lw['sgu_bt'], lw['w_read'], lw['w_o'],
               lw['ln_g'], lw['ln_b'])
    in_specs = [pl.BlockSpec(memory_space=pltpu.SMEM),
                pl.BlockSpec((1, TB, D_MODEL), lambda b, j: (b, j, 0))]
    in_specs += [_const_spec(w.shape) for w in weights]
    out_shape = (jax.ShapeDtypeStruct((nb, t, D_MODEL), F32),
                 jax.ShapeDtypeStruct((nb, TB, 2 * LANES), F32),
                 jax.ShapeDtypeStruct((nb, SUBLANES, 2 * SSM_W), F32))
    out_specs = (pl.BlockSpec((1, TB, D_MODEL), lambda b, j: (b, j, 0)),
                 pl.BlockSpec((1, TB, 2 * LANES), lambda b, j: (b, 0, 0)),
                 pl.BlockSpec((1, SUBLANES, 2 * SSM_W), lambda b, j: (b, 0, 0)))
    scratch = [pltpu.VMEM((TB, N_IN), F32),
               pltpu.VMEM((TB, N_BRANCH * BRANCH_W), F32),
               pltpu.VMEM((TB, 2 * LANES), F32),
               pltpu.VMEM((N_HEADS, TB, 2 * TB), F32),
               pltpu.VMEM((SCAN_STEPS, SUBLANES, 2 * SSM_W), F32),
               pltpu.VMEM((SCAN_PAD + TB, 2 * SSM_W), F32),
               pltpu.VMEM((SCAN_PAD + TB, 2 * SSM_W), F32),
               pltpu.VMEM((SUBLANES, 2 * SSM_W), F32)]
    return pl.pallas_call(
        _prompt_kernel,
        grid=grid,
        in_specs=in_specs,
        out_specs=out_specs,
        out_shape=out_shape,
        scratch_shapes=scratch,
        compiler_params=pltpu.CompilerParams(
            dimension_semantics=("arbitrary", "arbitrary"),
            vmem_limit_bytes=VMEM_LIMIT_BYTES),
        name="prompt_layer",
    )(sinks_p, x, *weights)


def _sample_kernel(x_ref, ck_ref, cv_ref, h0_ref, sink8_ref, w_in_ref, arow_ref, bb_ref, cc_ref, dskip_ref,
                   gluw_ref, glub_ref, sglng_ref, sglnb_ref, sgw0_ref, sgb0_ref, wread_ref, wo_ref,
                   lng_ref, lnb_ref,
                   y_ref, kv_ref, hout_ref, vn_ref,
                   proj_ref, ybr_ref):
    n_seq = x_ref.shape[0]
    win = ck_ref.shape[1]
    x = x_ref[...]
    _project(x, w_in_ref, proj_ref)
    kv_ref[...] = proj_ref[:, K0:K0 + 2 * LANES]

    row = lax.broadcasted_iota(jnp.int32, (SUBLANES, LANES), 0)
    lane = lax.broadcasted_iota(jnp.int32, (SUBLANES, LANES), 1)
    lane_lo = lane < HEAD_DIM
    row_even = (row % 2) == 0
    sel = jnp.logical_or(jnp.logical_and(row_even, lane_lo),
                         jnp.logical_and(jnp.logical_not(row_even), jnp.logical_not(lane_lo)))
    head = jnp.where(row_even, row // 2, N_HEADS // 2 + row // 2).astype(F32)
    slope = jnp.exp2(-(head + 1.0))
    bias = -slope * (win - lane).astype(F32)
    sink = sink8_ref[...]

    def seq_group_step(gi, carry):
        r0 = pl.multiple_of(gi * SUBLANES, SUBLANES)
        q_rows = proj_ref[pl.ds(r0, SUBLANES), Q0:Q0 + BRANCH_W] * ATTN_SCALE
        k_rows = proj_ref[pl.ds(r0, SUBLANES), K0:K0 + LANES]
        v_rows = proj_ref[pl.ds(r0, SUBLANES), V0:V0 + LANES]
        outs = [[] for _ in range(N_HEADS // 2)]
        for r in range(SUBLANES):
            n = r0 + r
            q8 = jnp.zeros((SUBLANES, LANES), F32)
            for t in range(N_HEADS // 2):
                qt = jnp.broadcast_to(q_rows[r:r + 1, t * LANES:(t + 1) * LANES], (SUBLANES, LANES))
                q8 = jnp.where(jnp.logical_and(sel, row // 2 == t), qt, q8)
            kn = ck_ref[n]
            vn_ = cv_ref[n]
            k_own = k_rows[r:r + 1, :]
            v_own = v_rows[r:r + 1, :]
            s = _dot_nt(q8.astype(BF16), kn.astype(BF16)) + bias
            s_own = jnp.sum(q8 * k_own, axis=-1, keepdims=True)
            m = jnp.maximum(jnp.maximum(jnp.max(s, axis=-1, keepdims=True), s_own), sink[:, 0:1])
            p = jnp.exp(s - m)
            p_own = jnp.exp(s_own - m)
            denom = jnp.sum(p, axis=-1, keepdims=True) + p_own + jnp.exp(sink[:, 0:1] - m)
            o = (_dot(p.astype(BF16), vn_.astype(BF16)) + p_own * v_own) * (1.0 / denom)
            o = jnp.where(sel, o, 0.0)
            for t in range(N_HEADS // 2):
                outs[t].append(o[2 * t:2 * t + 1, :] + o[2 * t + 1:2 * t + 2, :])
        for t in range(N_HEADS // 2):
            ybr_ref[pl.ds(r0, SUBLANES), t * LANES:(t + 1) * LANES] = jnp.concatenate(outs[t], axis=0)
        return carry

    lax.fori_loop(0, n_seq // SUBLANES, seq_group_step, 0)

    vn = _layernorm(proj_ref[:, VC0:VC0 + BRANCH_W], sglng_ref[...], sglnb_ref[...])
    vn_ref[...] = vn
    ybr_ref[:, 2 * BRANCH_W:3 * BRANCH_W] = proj_ref[:, UC0:UC0 + BRANCH_W] * (sgw0_ref[...] * vn + sgb0_ref[...])

    u = proj_ref[:, UB0:UB0 + BRANCH_W]
    xs = _dot(u.astype(BF16), bb_ref[...])
    a_r = arow_ref[:, 0:SSM_W]
    a_i = arow_ref[:, SSM_W:2 * SSM_W]
    h0r = h0_ref[:, 0:SSM_W]
    h0i = h0_ref[:, SSM_W:2 * SSM_W]
    hr = xs[:, 0:SSM_W] + (a_r * h0r - a_i * h0i)
    hi = xs[:, SSM_W:2 * SSM_W] + (a_r * h0i + a_i * h0r)
    hout_ref[:, 0:SSM_W] = hr
    hout_ref[:, SSM_W:2 * SSM_W] = hi
    h_bf = jnp.concatenate([hr, hi], axis=1).astype(BF16)
    ybr_ref[:, BRANCH_W:2 * BRANCH_W] = _s5_output(h_bf, u, cc_ref, dskip_ref, gluw_ref, glub_ref)

    y_ref[...] = _merge(x, proj_ref, ybr_ref, wread_ref, wo_ref, lng_ref, lnb_ref)


def _sample_layer(x, ck, cv, h0, sink8, lw):
    n = x.shape[0]
    out_shape = (jax.ShapeDtypeStruct((n, D_MODEL), F32),
                 jax.ShapeDtypeStruct((n, 2 * LANES), F32),
                 jax.ShapeDtypeStruct((n, 2 * SSM_W), F32),
                 jax.ShapeDtypeStruct((n, BRANCH_W), F32))
    scratch = [pltpu.VMEM((n, N_IN), F32), pltpu.VMEM((n, N_BRANCH * BRANCH_W), F32)]
    return pl.pallas_call(
        _sample_kernel,
        out_shape=out_shape,
        scratch_shapes=scratch,
        compiler_params=pltpu.CompilerParams(vmem_limit_bytes=VMEM_LIMIT_BYTES),
        name="sample_layer",
    )(x, ck, cv, h0, sink8, lw['w_in'], lw['a_row'], lw['bb'], lw['cc'], lw['d_skip'], lw['glu_w'],
      lw['glu_b'], lw['sgu_ln_g'], lw['sgu_ln_b'], lw['sgu_w0'], lw['sgu_b0'], lw['w_read'], lw['w_o'],
      lw['ln_g'], lw['ln_b'])


def _permute_head_blocks(w, axis):
    shape = w.shape
    split = shape[:axis] + (N_HEADS, HEAD_DIM) + shape[axis + 1:]
    return jnp.take(w.reshape(split), jnp.array(HEAD_PERM), axis=axis).reshape(shape)


def _layer_weights(l, w_in, attn_sinks, ssm_lambda_re, ssm_lambda_im, ssm_log_dt, ssm_b_re, ssm_b_im,
                   ssm_c_re, ssm_c_im, ssm_d, glu_w, glu_b, sgu_ln_g, sgu_ln_b, sgu_w, sgu_b,
                   w_read, w_o, ln_g, ln_b):
    w = w_in[l]
    w = jnp.concatenate([_permute_head_blocks(w[:, Q0:K0], 1), w[:, K0:ZA0],
                         _permute_head_blocks(w[:, ZA0:UB0], 1), w[:, UB0:]], axis=1).astype(BF16)
    wr = w_read[l]
    wr = jnp.concatenate([_permute_head_blocks(wr[0], 0)[None], wr[1:]], axis=0).astype(BF16)
    ar, ai, bbr, bbi = _ssm_prep(ssm_lambda_re[l], ssm_lambda_im[l], ssm_log_dt[l], ssm_b_re[l], ssm_b_im[l])
    a_row, bb, cc = _ssm_matrices(ar, ai, bbr, bbi, ssm_c_re[l], ssm_c_im[l])
    sinks_p = attn_sinks[l]
    sink8 = jnp.broadcast_to(sinks_p[jnp.array(HEAD_PERM)][:, None], (N_HEADS, LANES))
    row = lambda v: v.reshape(1, -1)
    return dict(
        w_in=w, a_row=a_row, bb=bb, cc=cc, d_skip=row(ssm_d[l]), glu_w=glu_w[l].astype(BF16),
        glu_b=row(glu_b[l]), sgu_ln_g=row(sgu_ln_g[l]), sgu_ln_b=row(sgu_ln_b[l]),
        sgu_w=sgu_w[l], sgu_bt=sgu_b[l].T,
        sgu_w0=row(jnp.repeat(sgu_w[l][:, 0, 0], BRANCH_W // SGU_GROUPS)),
        sgu_b0=row(jnp.repeat(sgu_b[l][:, 0], BRANCH_W // SGU_GROUPS)),
        w_read=wr, w_o=w_o[l].astype(BF16), ln_g=row(ln_g[l]), ln_b=row(ln_b[l]),
        sinks=sinks_p, sink8=sink8)


def kernel(x_prompt, x_sample, cache_k_win, cache_v_win, state_ssm_re, state_ssm_im, w_in, attn_sinks,
           ssm_lambda_re, ssm_lambda_im, ssm_log_dt, ssm_b_re, ssm_b_im, ssm_c_re, ssm_c_im, ssm_d,
           glu_w, glu_b, sgu_ln_g, sgu_ln_b, sgu_w, sgu_b, w_read, w_o, ln_g, ln_b):
    nb, t, _ = x_prompt.shape
    ns = x_sample.shape[0]
    win = cache_k_win.shape[2]
    xp = x_prompt
    xs = x_sample.reshape(ns, D_MODEL)
    kp, vp, hrp, hip = [], [], [], []
    ksm, vsm, hrs, his, vcs = [], [], [], [], []
    for l in range(DEPTH):
        lw = _layer_weights(l, w_in, attn_sinks, ssm_lambda_re, ssm_lambda_im, ssm_log_dt, ssm_b_re,
                            ssm_b_im, ssm_c_re, ssm_c_im, ssm_d, glu_w, glu_b, sgu_ln_g, sgu_ln_b,
                            sgu_w, sgu_b, w_read, w_o, ln_g, ln_b)
        xp, kvwin, hstate = _prompt_layer(xp, lw['sinks'], lw)
        kp.append(kvwin[:, :, 0:LANES].reshape(nb, TB, KV_HEADS, HEAD_DIM))
        vp.append(kvwin[:, :, LANES:].reshape(nb, TB, KV_HEADS, HEAD_DIM))
        hlast = hstate[:, SUBLANES - 1, :]
        hrp.append(hlast[:, 0:SSM_W].reshape(nb, SSM_GROUPS, SSM_STATE))
        hip.append(hlast[:, SSM_W:].reshape(nb, SSM_GROUPS, SSM_STATE))

        ck = cache_k_win[l].reshape(ns, win, KV_HEADS * HEAD_DIM)
        cv = cache_v_win[l].reshape(ns, win, KV_HEADS * HEAD_DIM)
        h0 = jnp.concatenate([state_ssm_re[l].reshape(ns, SSM_W), state_ssm_im[l].reshape(ns, SSM_W)], axis=1)
        xs, kv_s, h_s, vn_s = _sample_layer(xs, ck, cv, h0, lw['sink8'], lw)
        ksm.append(kv_s[:, 0:LANES].reshape(ns, 1, KV_HEADS, HEAD_DIM))
        vsm.append(kv_s[:, LANES:].reshape(ns, 1, KV_HEADS, HEAD_DIM))
        hrs.append(h_s[:, 0:SSM_W].reshape(ns, SSM_GROUPS, SSM_STATE))
        his.append(h_s[:, SSM_W:].reshape(ns, SSM_GROUPS, SSM_STATE))
        vcs.append(vn_s.reshape(ns, 1, BRANCH_W))
    return (xp, xs.reshape(ns, 1, D_MODEL), jnp.stack(kp), jnp.stack(vp), jnp.stack(hrp), jnp.stack(hip),
            jnp.stack(ksm), jnp.stack(vsm), jnp.stack(hrs), jnp.stack(his), jnp.stack(vcs))
```

```python
import functools
import math

import jax
import jax.numpy as jnp
from jax import lax
from jax.experimental import pallas as pl
from jax.experimental.pallas import tpu as pltpu

D_MODEL = 1024
BRANCH_W = 512
HEAD_DIM = 64
N_HEADS = 8
KV_HEADS = 2
WINDOW = 128
SSM_GROUP_CH = 16
SSM_GROUPS = 32
SSM_STATE = 64
SSM_W = SSM_GROUPS * SSM_STATE
CHUNK = 128
SGU_GROUPS = 4
N_BRANCH = 3
DEPTH = 2
ALPHA = (2 * DEPTH) ** 0.25
LN_EPS = 1e-5
ATTN_SCALE = HEAD_DIM ** -0.5
NEG_INF = -1e30

Q0, K0, V0, ZA0, UB0, ZB0, UC0, VC0, ZC0, G0 = 0, 512, 640, 768, 1280, 1792, 2304, 2816, 3328, 3840
N_IN = 6912

LANES = 128
SUBLANES = 8
VMEM_LIMIT_BYTES = 60 * 1024 * 1024

TB = 128
PROJ_CT = 768

SSM_TILES = BRANCH_W // LANES
SSM_HALF = SSM_W // SSM_TILES
SSM_TILE_W = 2 * SSM_HALF
SCAN_CHUNKS = SUBLANES
SCAN_LEN = TB // SCAN_CHUNKS


def _re(j):
    return slice(j * SSM_TILE_W, j * SSM_TILE_W + SSM_HALF)


def _im(j):
    return slice(j * SSM_TILE_W + SSM_HALF, (j + 1) * SSM_TILE_W)

HEAD_PERM = (0, 4, 1, 5, 2, 6, 3, 7)

F32 = jnp.float32
BF16 = jnp.bfloat16


def _sigmoid(x):
    return 1.0 / (1.0 + jnp.exp(-x))


def _silu(x):
    return x * _sigmoid(x)


def _gelu_tanh(x):
    c = math.sqrt(2.0 / math.pi)
    return 0.5 * x * (1.0 + jnp.tanh(c * (x + 0.044715 * (x * x * x))))


def _layernorm(x, g, b):
    mu = jnp.mean(x, axis=-1, keepdims=True)
    xc = x - mu
    var = jnp.mean(xc * xc, axis=-1, keepdims=True)
    return xc * lax.rsqrt(var + LN_EPS) * g + b


def _dot(a, b):
    return jnp.dot(a, b, preferred_element_type=F32)


def _dot_nt(a, b):
    return lax.dot_general(a, b, (((1,), (1,)), ((), ())), preferred_element_type=F32)


def _slope(h):
    return 2.0 ** (-(h + 1.0))


def _ssm_prep_kernel(lr_ref, li_ref, ldt_ref, brt_ref, bit_ref, ar_ref, ai_ref, bbr_ref, bbi_ref):
    lr = lr_ref[...]
    li = li_ref[...]
    dt = jnp.exp(ldt_ref[...])
    mag = jnp.exp(lr * dt)
    ar = mag * jnp.cos(li * dt)
    ai = mag * jnp.sin(li * dt)
    den = lr * lr + li * li
    cr = ((ar - 1.0) * lr + ai * li) / den
    ci = (ai * lr - (ar - 1.0) * li) / den
    ar_ref[...] = ar
    ai_ref[...] = ai
    for h in range(SSM_GROUP_CH):
        br = brt_ref[h]
        bi = bit_ref[h]
        bbr_ref[h] = cr * br - ci * bi
        bbi_ref[h] = cr * bi + ci * br


def _ssm_prep(lam_re, lam_im, log_dt, b_re, b_im):
    brt = jnp.transpose(b_re, (2, 0, 1))
    bit = jnp.transpose(b_im, (2, 0, 1))
    gp = jax.ShapeDtypeStruct((SSM_GROUPS, SSM_STATE), F32)
    hgp = jax.ShapeDtypeStruct((SSM_GROUP_CH, SSM_GROUPS, SSM_STATE), F32)
    return pl.pallas_call(
        _ssm_prep_kernel,
        out_shape=(gp, gp, hgp, hgp),
        name="ssm_prep",
    )(lam_re, lam_im, log_dt.reshape(SSM_GROUPS, 1), brt, bit)


def _ssm_matrices(ar, ai, bbr, bbi, c_re, c_im):
    gl = SSM_GROUPS // SSM_TILES
    eye = jnp.eye(gl, dtype=F32)

    def expand_b(b):
        b4 = b.reshape(SSM_GROUP_CH, SSM_TILES, gl, SSM_STATE)
        return jnp.einsum('hjgp,gk->jghkp', b4, eye).reshape(SSM_TILES, LANES, SSM_HALF)

    def expand_c(c):
        c4 = c.reshape(SSM_TILES, gl, SSM_GROUP_CH, SSM_STATE)
        return jnp.einsum('jghp,gk->jgpkh', c4, eye).reshape(SSM_TILES, SSM_HALF, LANES)

    bb = jnp.concatenate([expand_b(bbr), expand_b(bbi)], axis=2).astype(BF16)
    cc = jnp.concatenate([expand_c(c_re), -expand_c(c_im)], axis=1).astype(BF16)
    a_row = jnp.stack([ar.reshape(SSM_TILES, SSM_HALF), ai.reshape(SSM_TILES, SSM_HALF)],
                      axis=1).reshape(1, 2 * SSM_W)
    return a_row, bb, cc


def _state_to_kernel_layout(h_re, h_im):
    n = h_re.shape[0]
    return jnp.stack([h_re.reshape(n, SSM_TILES, SSM_HALF), h_im.reshape(n, SSM_TILES, SSM_HALF)],
                     axis=2).reshape(n, 2 * SSM_W)


def _state_from_kernel_layout(h):
    n = h.shape[0]
    h4 = h.reshape(n, SSM_TILES, 2, SSM_HALF)
    return (h4[:, :, 0].reshape(n, SSM_GROUPS, SSM_STATE), h4[:, :, 1].reshape(n, SSM_GROUPS, SSM_STATE))


def _project(x, w_in_ref, proj_ref):
    xb = x.astype(BF16)
    for c0 in range(0, N_IN, PROJ_CT):
        proj_ref[:, c0:c0 + PROJ_CT] = _dot(xb, w_in_ref[:, c0:c0 + PROJ_CT])


def _s5_output(y_lin, u, dskip_ref, gluw_ref, glub_ref):
    y = y_lin + dskip_ref[...] * u
    y = _gelu_tanh(y)
    return y * _sigmoid(_dot(y.astype(BF16), gluw_ref[...]) + glub_ref[...])


def _merge(x, proj_ref, ybr_ref, wread_ref, wo_ref, lng_ref, lnb_ref):
    merged = None
    for b, z0 in enumerate((ZA0, ZB0, ZC0)):
        yb = ybr_ref[:, b * BRANCH_W:(b + 1) * BRANCH_W] * _silu(proj_ref[:, z0:z0 + BRANCH_W])
        branch = _dot(yb.astype(BF16), wread_ref[b])
        gate = _sigmoid(proj_ref[:, G0 + b * D_MODEL:G0 + (b + 1) * D_MODEL])
        term = gate * branch
        merged = term if merged is None else merged + term
    out = _dot(merged.astype(BF16), wo_ref[...])
    return _layernorm(ALPHA * x + out, lng_ref[...], lnb_ref[...])


def _prompt_kernel(sinks_ref, x_ref, w_in_ref, arow_ref, bb_ref, cc_ref, dskip_ref, gluw_ref, glub_ref,
                   sglng_ref, sglnb_ref, sguw_ref, sgubt_ref, wread_ref, wo_ref, lng_ref, lnb_ref,
                   y_ref, kvwin_ref, hstate_ref,
                   proj_ref, ybr_ref, kvprev_ref, bias_ref, atab_ref, apw_ref, hs_ref, hcarry_ref,
                   uslab_ref, yslab_ref):
    bi = pl.program_id(0)
    ji = pl.program_id(1)

    @pl.when(jnp.logical_and(bi == 0, ji == 0))
    def _init_tables():
        qa = lax.broadcasted_iota(jnp.int32, (TB, 2 * TB), 0)
        kk = lax.broadcasted_iota(jnp.int32, (TB, 2 * TB), 1)
        dist = qa + WINDOW - kk
        valid = jnp.logical_and(dist >= 0, dist <= WINDOW)
        distf = dist.astype(F32)
        for h in range(N_HEADS):
            bias_ref[h] = jnp.where(valid, -_slope(h) * distf, NEG_INF)
        for j in range(SSM_TILES):
            a_r = arow_ref[:, _re(j)]
            a_i = arow_ref[:, _im(j)]
            pr, pi = a_r, a_i
            for s in range(SCAN_LEN):
                atab_ref[s, :, _re(j)] = jnp.broadcast_to(pr, (SUBLANES, SSM_HALF))
                atab_ref[s, :, _im(j)] = jnp.broadcast_to(pi, (SUBLANES, SSM_HALF))
                if s + 1 < SCAN_LEN:
                    pr, pi = pr * a_r - pi * a_i, pr * a_i + pi * a_r
            for k in range(2):
                pr, pi = pr * pr - pi * pi, 2.0 * (pr * pi)
                apw_ref[k, :, _re(j)] = jnp.broadcast_to(pr, (SUBLANES, SSM_HALF))
                apw_ref[k, :, _im(j)] = jnp.broadcast_to(pi, (SUBLANES, SSM_HALF))

    @pl.when(ji == 0)
    def _reset_carries():
        kvprev_ref[...] = jnp.zeros((TB, 2 * LANES), F32)
        hcarry_ref[...] = jnp.zeros((SUBLANES, 2 * SSM_W), F32)

    x = x_ref[0]
    _project(x, w_in_ref, proj_ref)

    kv_cur = proj_ref[:, K0:K0 + 2 * LANES]
    kv_prev = kvprev_ref[...]
    kcat = jnp.concatenate([kv_prev[:, 0:LANES], kv_cur[:, 0:LANES]], axis=0).astype(BF16)
    vcat = jnp.concatenate([kv_prev[:, LANES:], kv_cur[:, LANES:]], axis=0).astype(BF16)
    first_neg = jnp.where(ji == 0, NEG_INF, 0.0).astype(F32)
    kcol = lax.broadcasted_iota(jnp.int32, (TB, 2 * TB), 1)
    prev_mask = jnp.where(kcol < TB, first_neg, 0.0)
    lane = lax.broadcasted_iota(jnp.int32, (TB, LANES), 1)
    lo = lane < HEAD_DIM
    for t in range(N_HEADS // 2):
        qt = proj_ref[:, Q0 + t * LANES:Q0 + (t + 1) * LANES] * ATTN_SCALE
        halves = []
        for half, h in ((0, t), (1, t + N_HEADS // 2)):
            qm = jnp.where(lo if half == 0 else jnp.logical_not(lo), qt, 0.0).astype(BF16)
            s = _dot_nt(qm, kcat) + bias_ref[h] + prev_mask
            sink = sinks_ref[h]
            m = jnp.maximum(jnp.max(s, axis=-1, keepdims=True), sink)
            p = jnp.exp(s - m)
            denom = jnp.sum(p, axis=-1, keepdims=True) + jnp.exp(sink - m)
            o = _dot(p.astype(BF16), vcat)
            halves.append(o * (1.0 / denom))
        ybr_ref[:, t * LANES:(t + 1) * LANES] = jnp.where(lo, halves[0], halves[1])
    kvprev_ref[...] = kv_cur
    kvwin_ref[0] = kv_cur

    vn = _layernorm(proj_ref[:, VC0:VC0 + BRANCH_W], sglng_ref[...], sglnb_ref[...])
    trow = lax.broadcasted_iota(jnp.int32, (CHUNK, CHUNK), 0)
    tcol = lax.broadcasted_iota(jnp.int32, (CHUNK, CHUNK), 1)
    tril = trow >= tcol
    for g in range(SGU_GROUPS):
        wm = jnp.where(tril, sguw_ref[g], 0.0).astype(BF16)
        sg = _dot(wm, vn[:, g * LANES:(g + 1) * LANES].astype(BF16)) + sgubt_ref[:, g:g + 1]
        ybr_ref[:, 2 * BRANCH_W + g * LANES:2 * BRANCH_W + (g + 1) * LANES] = (
            proj_ref[:, UC0 + g * LANES:UC0 + (g + 1) * LANES] * sg)

    u = proj_ref[:, UB0:UB0 + BRANCH_W]
    for j in range(SSM_TILES):
        uslab_ref[j] = u[:, j * LANES:(j + 1) * LANES]
    for j in range(SSM_TILES):
        u_perm = jnp.concatenate(
            [uslab_ref[j, pl.ds(s, SCAN_CHUNKS, stride=SCAN_LEN), :] for s in range(SCAN_LEN)], axis=0)
        hs_ref[:, j * SSM_TILE_W:(j + 1) * SSM_TILE_W] = _dot(u_perm.astype(BF16), bb_ref[j])
    srow = lax.broadcasted_iota(jnp.int32, (SUBLANES, SSM_HALF), 0)
    for j in range(SSM_TILES):
        re, im = _re(j), _im(j)
        a_r = atab_ref[0, :, re]
        a_i = atab_ref[0, :, im]
        c_r = jnp.broadcast_to(hcarry_ref[SUBLANES - 1:SUBLANES, re], (SUBLANES, SSM_HALF))
        c_i = jnp.broadcast_to(hcarry_ref[SUBLANES - 1:SUBLANES, im], (SUBLANES, SSM_HALF))
        h_r = hs_ref[0:SUBLANES, re] + jnp.where(srow == 0, a_r * c_r - a_i * c_i, 0.0)
        h_i = hs_ref[0:SUBLANES, im] + jnp.where(srow == 0, a_r * c_i + a_i * c_r, 0.0)
        hs_ref[0:SUBLANES, re] = h_r
        hs_ref[0:SUBLANES, im] = h_i
        for s in range(1, SCAN_LEN):
            rows = slice(s * SUBLANES, (s + 1) * SUBLANES)
            h_r, h_i = (hs_ref[rows, re] + (a_r * h_r - a_i * h_i),
                        hs_ref[rows, im] + (a_r * h_i + a_i * h_r))
            hs_ref[rows, re] = h_r
            hs_ref[rows, im] = h_i
        e_r, e_i = h_r, h_i
        for k, d in enumerate((1, 2, 4)):
            if k == 0:
                p_r, p_i = atab_ref[SCAN_LEN - 1, :, re], atab_ref[SCAN_LEN - 1, :, im]
            else:
                p_r, p_i = apw_ref[k - 1, :, re], apw_ref[k - 1, :, im]
            s_r = pltpu.roll(e_r, d, 0)
            s_i = pltpu.roll(e_i, d, 0)
            e_r, e_i = (e_r + jnp.where(srow >= d, p_r * s_r - p_i * s_i, 0.0),
                        e_i + jnp.where(srow >= d, p_r * s_i + p_i * s_r, 0.0))
        hcarry_ref[:, re] = e_r
        hcarry_ref[:, im] = e_i
        hstate_ref[0, :, re] = e_r
        hstate_ref[0, :, im] = e_i
        in_r = jnp.where(srow == 0, 0.0, pltpu.roll(e_r, 1, 0))
        in_i = jnp.where(srow == 0, 0.0, pltpu.roll(e_i, 1, 0))
        for s in range(SCAN_LEN):
            rows = slice(s * SUBLANES, (s + 1) * SUBLANES)
            t_r = atab_ref[s, :, re]
            t_i = atab_ref[s, :, im]
            hs_ref[rows, re] = hs_ref[rows, re] + (t_r * in_r - t_i * in_i)
            hs_ref[rows, im] = hs_ref[rows, im] + (t_r * in_i + t_i * in_r)
        y_perm = _dot(hs_ref[:, j * SSM_TILE_W:(j + 1) * SSM_TILE_W].astype(BF16), cc_ref[j])
        for s in range(SCAN_LEN):
            yslab_ref[j, pl.ds(s, SCAN_CHUNKS, stride=SCAN_LEN), :] = y_perm[s * SUBLANES:(s + 1) * SUBLANES, :]
    y_lin = jnp.concatenate([yslab_ref[j] for j in range(SSM_TILES)], axis=1)
    ybr_ref[:, BRANCH_W:2 * BRANCH_W] = _s5_output(y_lin, u, dskip_ref, gluw_ref, glub_ref)

    y_ref[0] = _merge(x, proj_ref, ybr_ref, wread_ref, wo_ref, lng_ref, lnb_ref)


def _const_spec(shape):
    nd = len(shape)
    return pl.BlockSpec(shape, lambda b, j, _nd=nd: (0,) * _nd, pipeline_mode=pl.Buffered(1))


def _prompt_layer(x, sinks_p, lw):
    nb, t, _ = x.shape
    grid = (nb, t // TB)
    weights = (lw['w_in'], lw['a_row'], lw['bb'], lw['cc'], lw['d_skip'], lw['glu_w'], lw['glu_b'],
               lw['sgu_ln_g'], lw['sgu_ln_b'], lw['sgu_w'], lw['sgu_bt'], lw['w_read'], lw['w_o'],
               lw['ln_g'], lw['ln_b'])
    in_specs = [pl.BlockSpec(memory_space=pltpu.SMEM),
                pl.BlockSpec((1, TB, D_MODEL), lambda b, j: (b, j, 0))]
    in_specs += [_const_spec(w.shape) for w in weights]
    out_shape = (jax.ShapeDtypeStruct((nb, t, D_MODEL), F32),
                 jax.ShapeDtypeStruct((nb, TB, 2 * LANES), F32),
                 jax.ShapeDtypeStruct((nb, SUBLANES, 2 * SSM_W), F32))
    out_specs = (pl.BlockSpec((1, TB, D_MODEL), lambda b, j: (b, j, 0)),
                 pl.BlockSpec((1, TB, 2 * LANES), lambda b, j: (b, 0, 0)),
                 pl.BlockSpec((1, SUBLANES, 2 * SSM_W), lambda b, j: (b, 0, 0)))
    scratch = [pltpu.VMEM((TB, N_IN), F32),
               pltpu.VMEM((TB, N_BRANCH * BRANCH_W), F32),
               pltpu.VMEM((TB, 2 * LANES), F32),
               pltpu.VMEM((N_HEADS, TB, 2 * TB), F32),
               pltpu.VMEM((SCAN_LEN, SUBLANES, 2 * SSM_W), F32),
               pltpu.VMEM((2, SUBLANES, 2 * SSM_W), F32),
               pltpu.VMEM((TB, 2 * SSM_W), F32),
               pltpu.VMEM((SUBLANES, 2 * SSM_W), F32),
               pltpu.VMEM((SSM_TILES, TB, LANES), F32),
               pltpu.VMEM((SSM_TILES, TB, LANES), F32)]
    return pl.pallas_call(
        _prompt_kernel,
        grid=grid,
        in_specs=in_specs,
        out_specs=out_specs,
        out_shape=out_shape,
        scratch_shapes=scratch,
        compiler_params=pltpu.CompilerParams(
            dimension_semantics=("arbitrary", "arbitrary"),
            vmem_limit_bytes=VMEM_LIMIT_BYTES),
        name="prompt_layer",
    )(sinks_p, x, *weights)


def _sample_kernel(x_ref, ck_ref, cv_ref, h0_ref, sink8_ref, w_in_ref, arow_ref, bb_ref, cc_ref, dskip_ref,
                   gluw_ref, glub_ref, sglng_ref, sglnb_ref, sgw0_ref, sgb0_ref, wread_ref, wo_ref,
                   lng_ref, lnb_ref,
                   y_ref, kv_ref, hout_ref, vn_ref,
                   proj_ref, ybr_ref):
    n_seq = x_ref.shape[0]
    win = ck_ref.shape[1]
    x = x_ref[...]
    _project(x, w_in_ref, proj_ref)
    kv_ref[...] = proj_ref[:, K0:K0 + 2 * LANES]

    row = lax.broadcasted_iota(jnp.int32, (SUBLANES, LANES), 0)
    lane = lax.broadcasted_iota(jnp.int32, (SUBLANES, LANES), 1)
    lane_lo = lane < HEAD_DIM
    row_even = (row % 2) == 0
    sel = jnp.logical_or(jnp.logical_and(row_even, lane_lo),
                         jnp.logical_and(jnp.logical_not(row_even), jnp.logical_not(lane_lo)))
    head = jnp.where(row_even, row // 2, N_HEADS // 2 + row // 2).astype(F32)
    slope = jnp.exp2(-(head + 1.0))
    bias = -slope * (win - lane).astype(F32)
    sink = sink8_ref[...]

    def seq_group_step(gi, carry):
        r0 = pl.multiple_of(gi * SUBLANES, SUBLANES)
        q_rows = proj_ref[pl.ds(r0, SUBLANES), Q0:Q0 + BRANCH_W] * ATTN_SCALE
        k_rows = proj_ref[pl.ds(r0, SUBLANES), K0:K0 + LANES]
        v_rows = proj_ref[pl.ds(r0, SUBLANES), V0:V0 + LANES]
        outs = [[] for _ in range(N_HEADS // 2)]
        for r in range(SUBLANES):
            n = r0 + r
            q8 = jnp.zeros((SUBLANES, LANES), F32)
            for t in range(N_HEADS // 2):
                qt = jnp.broadcast_to(q_rows[r:r + 1, t * LANES:(t + 1) * LANES], (SUBLANES, LANES))
                q8 = jnp.where(jnp.logical_and(sel, row // 2 == t), qt, q8)
            kn = ck_ref[n]
            vn_ = cv_ref[n]
            k_own = k_rows[r:r + 1, :]
            v_own = v_rows[r:r + 1, :]
            s = _dot_nt(q8.astype(BF16), kn.astype(BF16)) + bias
            s_own = jnp.sum(q8 * k_own, axis=-1, keepdims=True)
            m = jnp.maximum(jnp.maximum(jnp.max(s, axis=-1, keepdims=True), s_own), sink[:, 0:1])
            p = jnp.exp(s - m)
            p_own = jnp.exp(s_own - m)
            denom = jnp.sum(p, axis=-1, keepdims=True) + p_own + jnp.exp(sink[:, 0:1] - m)
            o = (_dot(p.astype(BF16), vn_.astype(BF16)) + p_own * v_own) * (1.0 / denom)
            o = jnp.where(sel, o, 0.0)
            for t in range(N_HEADS // 2):
                outs[t].append(o[2 * t:2 * t + 1, :] + o[2 * t + 1:2 * t + 2, :])
        for t in range(N_HEADS // 2):
            ybr_ref[pl.ds(r0, SUBLANES), t * LANES:(t + 1) * LANES] = jnp.concatenate(outs[t], axis=0)
        return carry

    lax.fori_loop(0, n_seq // SUBLANES, seq_group_step, 0)

    vn = _layernorm(proj_ref[:, VC0:VC0 + BRANCH_W], sglng_ref[...], sglnb_ref[...])
    vn_ref[...] = vn
    ybr_ref[:, 2 * BRANCH_W:3 * BRANCH_W] = proj_ref[:, UC0:UC0 + BRANCH_W] * (sgw0_ref[...] * vn + sgb0_ref[...])

    u = proj_ref[:, UB0:UB0 + BRANCH_W]
    y_tiles = []
    for j in range(SSM_TILES):
        re, im = _re(j), _im(j)
        xs = _dot(u[:, j * LANES:(j + 1) * LANES].astype(BF16), bb_ref[j])
        a_r = arow_ref[:, re]
        a_i = arow_ref[:, im]
        h0r = h0_ref[:, re]
        h0i = h0_ref[:, im]
        hr = xs[:, 0:SSM_HALF] + (a_r * h0r - a_i * h0i)
        hi = xs[:, SSM_HALF:] + (a_r * h0i + a_i * h0r)
        hout_ref[:, re] = hr
        hout_ref[:, im] = hi
        y_tiles.append(_dot(jnp.concatenate([hr, hi], axis=1).astype(BF16), cc_ref[j]))
    y_lin = jnp.concatenate(y_tiles, axis=1)
    ybr_ref[:, BRANCH_W:2 * BRANCH_W] = _s5_output(y_lin, u, dskip_ref, gluw_ref, glub_ref)

    y_ref[...] = _merge(x, proj_ref, ybr_ref, wread_ref, wo_ref, lng_ref, lnb_ref)


def _sample_layer(x, ck, cv, h0, sink8, lw):
    n = x.shape[0]
    out_shape = (jax.ShapeDtypeStruct((n, D_MODEL), F32),
                 jax.ShapeDtypeStruct((n, 2 * LANES), F32),
                 jax.ShapeDtypeStruct((n, 2 * SSM_W), F32),
                 jax.ShapeDtypeStruct((n, BRANCH_W), F32))
    scratch = [pltpu.VMEM((n, N_IN), F32), pltpu.VMEM((n, N_BRANCH * BRANCH_W), F32)]
    return pl.pallas_call(
        _sample_kernel,
        out_shape=out_shape,
        scratch_shapes=scratch,
        compiler_params=pltpu.CompilerParams(vmem_limit_bytes=VMEM_LIMIT_BYTES),
        name="sample_layer",
    )(x, ck, cv, h0, sink8, lw['w_in'], lw['a_row'], lw['bb'], lw['cc'], lw['d_skip'], lw['glu_w'],
      lw['glu_b'], lw['sgu_ln_g'], lw['sgu_ln_b'], lw['sgu_w0'], lw['sgu_b0'], lw['w_read'], lw['w_o'],
      lw['ln_g'], lw['ln_b'])


def _permute_head_blocks(w, axis):
    shape = w.shape
    split = shape[:axis] + (N_HEADS, HEAD_DIM) + shape[axis + 1:]
    return jnp.take(w.reshape(split), jnp.array(HEAD_PERM), axis=axis).reshape(shape)


def _layer_weights(l, w_in, attn_sinks, ssm_lambda_re, ssm_lambda_im, ssm_log_dt, ssm_b_re, ssm_b_im,
                   ssm_c_re, ssm_c_im, ssm_d, glu_w, glu_b, sgu_ln_g, sgu_ln_b, sgu_w, sgu_b,
                   w_read, w_o, ln_g, ln_b):
    w = w_in[l]
    w = jnp.concatenate([_permute_head_blocks(w[:, Q0:K0], 1), w[:, K0:ZA0],
                         _permute_head_blocks(w[:, ZA0:UB0], 1), w[:, UB0:]], axis=1).astype(BF16)
    wr = w_read[l]
    wr = jnp.concatenate([_permute_head_blocks(wr[0], 0)[None], wr[1:]], axis=0).astype(BF16)
    ar, ai, bbr, bbi = _ssm_prep(ssm_lambda_re[l], ssm_lambda_im[l], ssm_log_dt[l], ssm_b_re[l], ssm_b_im[l])
    a_row, bb, cc = _ssm_matrices(ar, ai, bbr, bbi, ssm_c_re[l], ssm_c_im[l])
    sinks_p = attn_sinks[l]
    sink8 = jnp.broadcast_to(sinks_p[jnp.array(HEAD_PERM)][:, None], (N_HEADS, LANES))
    row = lambda v: v.reshape(1, -1)
    return dict(
        w_in=w, a_row=a_row, bb=bb, cc=cc, d_skip=row(ssm_d[l]), glu_w=glu_w[l].astype(BF16),
        glu_b=row(glu_b[l]), sgu_ln_g=row(sgu_ln_g[l]), sgu_ln_b=row(sgu_ln_b[l]),
        sgu_w=sgu_w[l], sgu_bt=sgu_b[l].T,
        sgu_w0=row(jnp.repeat(sgu_w[l][:, 0, 0], BRANCH_W // SGU_GROUPS)),
        sgu_b0=row(jnp.repeat(sgu_b[l][:, 0], BRANCH_W // SGU_GROUPS)),
        w_read=wr, w_o=w_o[l].astype(BF16), ln_g=row(ln_g[l]), ln_b=row(ln_b[l]),
        sinks=sinks_p, sink8=sink8)


def kernel(x_prompt, x_sample, cache_k_win, cache_v_win, state_ssm_re, state_ssm_im, w_in, attn_sinks,
           ssm_lambda_re, ssm_lambda_im, ssm_log_dt, ssm_b_re, ssm_b_im, ssm_c_re, ssm_c_im, ssm_d,
           glu_w, glu_b, sgu_ln_g, sgu_ln_b, sgu_w, sgu_b, w_read, w_o, ln_g, ln_b):
    nb, t, _ = x_prompt.shape
    ns = x_sample.shape[0]
    win = cache_k_win.shape[2]
    xp = x_prompt
    xs = x_sample.reshape(ns, D_MODEL)
    kp, vp, hrp, hip = [], [], [], []
    ksm, vsm, hrs, his, vcs = [], [], [], [], []
    for l in range(DEPTH):
        lw = _layer_weights(l, w_in, attn_sinks, ssm_lambda_re, ssm_lambda_im, ssm_log_dt, ssm_b_re,
                            ssm_b_im, ssm_c_re, ssm_c_im, ssm_d, glu_w, glu_b, sgu_ln_g, sgu_ln_b,
                            sgu_w, sgu_b, w_read, w_o, ln_g, ln_b)
        xp, kvwin, hstate = _prompt_layer(xp, lw['sinks'], lw)
        kp.append(kvwin[:, :, 0:LANES].reshape(nb, TB, KV_HEADS, HEAD_DIM))
        vp.append(kvwin[:, :, LANES:].reshape(nb, TB, KV_HEADS, HEAD_DIM))
        h_re, h_im = _state_from_kernel_layout(hstate[:, SUBLANES - 1, :])
        hrp.append(h_re)
        hip.append(h_im)

        ck = cache_k_win[l].reshape(ns, win, KV_HEADS * HEAD_DIM)
        cv = cache_v_win[l].reshape(ns, win, KV_HEADS * HEAD_DIM)
        h0 = _state_to_kernel_layout(state_ssm_re[l], state_ssm_im[l])
        xs, kv_s, h_s, vn_s = _sample_layer(xs, ck, cv, h0, lw['sink8'], lw)
        ksm.append(kv_s[:, 0:LANES].reshape(ns, 1, KV_HEADS, HEAD_DIM))
        vsm.append(kv_s[:, LANES:].reshape(ns, 1, KV_HEADS, HEAD_DIM))
        h_re, h_im = _state_from_kernel_layout(h_s)
        hrs.append(h_re)
        his.append(h_im)
        vcs.append(vn_s.reshape(ns, 1, BRANCH_W))
    return (xp, xs.reshape(ns, 1, D_MODEL), jnp.stack(kp), jnp.stack(vp), jnp.stack(hrp), jnp.stack(hip),
            jnp.stack(ksm), jnp.stack(vsm), jnp.stack(hrs), jnp.stack(his), jnp.stack(vcs))
```

```python
import functools
import math

import jax
import jax.numpy as jnp
from jax import lax
from jax.experimental import pallas as pl
from jax.experimental.pallas import tpu as pltpu

D_MODEL = 1024
BRANCH_W = 512
HEAD_DIM = 64
N_HEADS = 8
KV_HEADS = 2
GQA_GROUP = N_HEADS // KV_HEADS
WINDOW = 128
SSM_GROUP_CH = 16
SSM_GROUPS = 32
SSM_STATE = 64
SSM_W = SSM_GROUPS * SSM_STATE
CHUNK = 128
SGU_GROUPS = 4
N_BRANCH = 3
DEPTH = 2
ALPHA = (2 * DEPTH) ** 0.25
LN_EPS = 1e-5
ATTN_SCALE = HEAD_DIM ** -0.5
NEG_INF = -1e30

Q0, K0, V0, ZA0, UB0, ZB0, UC0, VC0, ZC0, G0 = 0, 512, 640, 768, 1280, 1792, 2304, 2816, 3328, 3840
N_IN = 6912

LANES = 128
SUBLANES = 8
MXU_COLS = 256
VMEM_LIMIT_BYTES = 60 * 1024 * 1024

TB = 128
PROJ_CT = MXU_COLS

SSM_TILES = BRANCH_W // LANES
SSM_TILE_GROUPS = SSM_GROUPS // SSM_TILES
SSM_HALF = SSM_W // SSM_TILES
SSM_TILE_W = 2 * SSM_HALF
SCAN_CHUNKS = SUBLANES
SCAN_LEN = TB // SCAN_CHUNKS

ROW_DSKIP_GLUB, ROW_SGU_LN, ROW_LN_G, ROW_LN_B, ROW_SGU_B = 0, 1, 2, 3, 4
PARAM_ROWS = SUBLANES

F32 = jnp.float32
BF16 = jnp.bfloat16


def _re(j):
    return slice(j * SSM_TILE_W, j * SSM_TILE_W + SSM_HALF)


def _im(j):
    return slice(j * SSM_TILE_W + SSM_HALF, (j + 1) * SSM_TILE_W)


def _sigmoid(x):
    return 1.0 / (1.0 + jnp.exp(-x))


def _silu(x):
    return x * _sigmoid(x)


def _gelu_tanh(x):
    c = math.sqrt(2.0 / math.pi)
    return 0.5 * x * (1.0 + jnp.tanh(c * (x + 0.044715 * (x * x * x))))


def _layernorm(x, g, b):
    mu = jnp.mean(x, axis=-1, keepdims=True)
    xc = x - mu
    var = jnp.mean(xc * xc, axis=-1, keepdims=True)
    return xc * lax.rsqrt(var + LN_EPS) * g + b


def _dot(a, b):
    return jnp.dot(a, b, preferred_element_type=F32)


def _dot_nt(a, b):
    return lax.dot_general(a, b, (((1,), (1,)), ((), ())), preferred_element_type=F32)


def _slope(h):
    return 2.0 ** (-(h + 1.0))


def _dup_halves(x, lo):
    sw = pltpu.roll(x, HEAD_DIM, 1)
    return jnp.where(lo, x, sw), jnp.where(lo, sw, x)


def _ssm_prep_kernel(lr_ref, li_ref, ldt_ref, brt_ref, bit_ref, crt_ref, cit_ref,
                     arow_ref, bb_ref, cc_ref, bbs_ref, ccs_ref):
    lr = lr_ref[...]
    li = li_ref[...]
    dt = jnp.exp(ldt_ref[...])
    mag = jnp.exp(lr * dt)
    ar = mag * jnp.cos(li * dt)
    ai = mag * jnp.sin(li * dt)
    den = lr * lr + li * li
    cr = ((ar - 1.0) * lr + ai * li) / den
    ci = (ai * lr - (ar - 1.0) * li) / den
    for j in range(SSM_TILES):
        bbs_ref[...] = jnp.zeros((LANES, SSM_TILE_W), F32)
        ccs_ref[...] = jnp.zeros((SSM_TILE_W, LANES), F32)
        for gl in range(SSM_TILE_GROUPS):
            g = j * SSM_TILE_GROUPS + gl
            cols = slice(gl * SSM_STATE, (gl + 1) * SSM_STATE)
            cols_im = slice(SSM_HALF + gl * SSM_STATE, SSM_HALF + (gl + 1) * SSM_STATE)
            ch = slice(gl * SSM_GROUP_CH, (gl + 1) * SSM_GROUP_CH)
            arow_ref[:, j * SSM_TILE_W + gl * SSM_STATE:j * SSM_TILE_W + (gl + 1) * SSM_STATE] = ar[g:g + 1, :]
            arow_ref[:, j * SSM_TILE_W + SSM_HALF + gl * SSM_STATE:
                     j * SSM_TILE_W + SSM_HALF + (gl + 1) * SSM_STATE] = ai[g:g + 1, :]
            br = brt_ref[g]
            bi = bit_ref[g]
            crg = cr[g:g + 1, :]
            cig = ci[g:g + 1, :]
            bbs_ref[ch, cols] = crg * br - cig * bi
            bbs_ref[ch, cols_im] = crg * bi + cig * br
            ccs_ref[cols, ch] = crt_ref[g]
            ccs_ref[cols_im, ch] = -cit_ref[g]
        bb_ref[j] = bbs_ref[...].astype(BF16)
        cc_ref[j] = ccs_ref[...].astype(BF16)


def _ssm_prep(lam_re, lam_im, log_dt, b_re, b_im, c_re, c_im):
    depth = lam_re.shape[0]
    brt = jnp.swapaxes(b_re, 2, 3)
    bit = jnp.swapaxes(b_im, 2, 3)
    crt = jnp.swapaxes(c_re, 2, 3)
    cit = jnp.swapaxes(c_im, 2, 3)

    def spec(*shape):
        nd = len(shape)
        return pl.BlockSpec((None,) + shape, lambda l, _nd=nd: (l,) + (0,) * _nd)

    return pl.pallas_call(
        _ssm_prep_kernel,
        grid=(depth,),
        in_specs=[spec(SSM_GROUPS, SSM_STATE), spec(SSM_GROUPS, SSM_STATE), spec(SSM_GROUPS, 1),
                  spec(SSM_GROUPS, SSM_GROUP_CH, SSM_STATE), spec(SSM_GROUPS, SSM_GROUP_CH, SSM_STATE),
                  spec(SSM_GROUPS, SSM_STATE, SSM_GROUP_CH), spec(SSM_GROUPS, SSM_STATE, SSM_GROUP_CH)],
        out_specs=(spec(1, 2 * SSM_W), spec(SSM_TILES, LANES, SSM_TILE_W), spec(SSM_TILES, SSM_TILE_W, LANES)),
        out_shape=(jax.ShapeDtypeStruct((depth, 1, 2 * SSM_W), F32),
                   jax.ShapeDtypeStruct((depth, SSM_TILES, LANES, SSM_TILE_W), BF16),
                   jax.ShapeDtypeStruct((depth, SSM_TILES, SSM_TILE_W, LANES), BF16)),
        scratch_shapes=[pltpu.VMEM((LANES, SSM_TILE_W), F32), pltpu.VMEM((SSM_TILE_W, LANES), F32)],
        compiler_params=pltpu.CompilerParams(dimension_semantics=("arbitrary",)),
        name="ssm_prep",
    )(lam_re, lam_im, log_dt.reshape(depth, SSM_GROUPS, 1), brt, bit, crt, cit)


def _state_to_kernel_layout(h_re, h_im):
    lead = h_re.shape[:-2]
    return jnp.stack([h_re.reshape(lead + (SSM_TILES, SSM_HALF)), h_im.reshape(lead + (SSM_TILES, SSM_HALF))],
                     axis=-2).reshape(lead + (2 * SSM_W,))


def _state_from_kernel_layout(h):
    n = h.shape[0]
    h4 = h.reshape(n, SSM_TILES, 2, SSM_HALF)
    return (h4[:, :, 0].reshape(n, SSM_GROUPS, SSM_STATE), h4[:, :, 1].reshape(n, SSM_GROUPS, SSM_STATE))


def _proj_chunk(xb, w_in_ref, proj_ref, c0):
    y = _dot(xb, w_in_ref[:, c0:c0 + PROJ_CT])
    if c0 >= G0:
        y = _sigmoid(y)
    elif any(z0 <= c0 < z0 + BRANCH_W for z0 in (ZA0, ZB0, ZC0)):
        y = _silu(y)
    elif c0 < K0:
        y = y * ATTN_SCALE
    proj_ref[:, c0:c0 + PROJ_CT] = y


def _s5_output(y_lin, u, ptab_ref, gluw_ref):
    y = y_lin + ptab_ref[ROW_DSKIP_GLUB:ROW_DSKIP_GLUB + 1, 0:BRANCH_W] * u
    y = _gelu_tanh(y)
    glu_b = ptab_ref[ROW_DSKIP_GLUB:ROW_DSKIP_GLUB + 1, BRANCH_W:2 * BRANCH_W]
    return y * _sigmoid(_dot(y.astype(BF16), gluw_ref[...]) + glu_b)


def _merge(x, proj_ref, ybr_ref, wread_ref, wo_ref, ptab_ref):
    merged = None
    for b, z0 in enumerate((ZA0, ZB0, ZC0)):
        yb = ybr_ref[:, b * BRANCH_W:(b + 1) * BRANCH_W] * proj_ref[:, z0:z0 + BRANCH_W]
        branch = _dot(yb.astype(BF16), wread_ref[b])
        term = proj_ref[:, G0 + b * D_MODEL:G0 + (b + 1) * D_MODEL] * branch
        merged = term if merged is None else merged + term
    out = _dot(merged.astype(BF16), wo_ref[...])
    return _layernorm(ALPHA * x + out, ptab_ref[ROW_LN_G:ROW_LN_G + 1, :], ptab_ref[ROW_LN_B:ROW_LN_B + 1, :])


def _sgu_layernorm(v, ptab_ref):
    return _layernorm(v, ptab_ref[ROW_SGU_LN:ROW_SGU_LN + 1, 0:BRANCH_W],
                      ptab_ref[ROW_SGU_LN:ROW_SGU_LN + 1, BRANCH_W:2 * BRANCH_W])


def _prompt_kernel(layer, sinks_ref, x_ref, w_in_ref, arow_ref, bb_ref, cc_ref, ptab_ref, gluw_ref,
                   sguw_ref, wread_ref, wo_ref,
                   y_ref, kvwin_ref, hstate_ref,
                   proj_ref, ybr_ref, kvprev_ref, bias_ref, atab_ref, apw_ref, hs_ref, hcarry_ref,
                   uslab_ref, yslab_ref):
    bi = pl.program_id(0)
    ji = pl.program_id(1)

    @pl.when(jnp.logical_and(bi == 0, ji == 0))
    def _init_tables():
        qa = lax.broadcasted_iota(jnp.int32, (TB, 2 * TB), 0)
        kk = lax.broadcasted_iota(jnp.int32, (TB, 2 * TB), 1)
        dist = qa + WINDOW - kk
        valid = jnp.logical_and(dist >= 0, dist <= WINDOW)
        distf = dist.astype(F32)
        for h in range(N_HEADS):
            bias_ref[h] = jnp.where(valid, -_slope(h) * distf, NEG_INF)
        for j in range(SSM_TILES):
            a_r = arow_ref[:, _re(j)]
            a_i = arow_ref[:, _im(j)]
            pr, pi = a_r, a_i
            for s in range(SCAN_LEN):
                atab_ref[s, :, _re(j)] = jnp.broadcast_to(pr, (SUBLANES, SSM_HALF))
                atab_ref[s, :, _im(j)] = jnp.broadcast_to(pi, (SUBLANES, SSM_HALF))
                if s + 1 < SCAN_LEN:
                    pr, pi = pr * a_r - pi * a_i, pr * a_i + pi * a_r
            for k in range(2):
                pr, pi = pr * pr - pi * pi, 2.0 * (pr * pi)
                apw_ref[k, :, _re(j)] = jnp.broadcast_to(pr, (SUBLANES, SSM_HALF))
                apw_ref[k, :, _im(j)] = jnp.broadcast_to(pi, (SUBLANES, SSM_HALF))

    @pl.when(ji == 0)
    def _reset_carries():
        kvprev_ref[...] = jnp.zeros((TB, 2 * LANES), F32)
        hcarry_ref[...] = jnp.zeros((SUBLANES, 2 * SSM_W), F32)

    xb = x_ref[0].astype(BF16)
    late_chunks = ([z0 + c for z0 in (ZA0, ZB0, ZC0) for c in range(0, BRANCH_W, PROJ_CT)]
                   + list(range(G0, N_IN, PROJ_CT)))

    def project_late(n):
        for _ in range(n):
            _proj_chunk(xb, w_in_ref, proj_ref, late_chunks.pop(0))

    for c0 in range(UB0, UB0 + BRANCH_W, PROJ_CT):
        _proj_chunk(xb, w_in_ref, proj_ref, c0)

    for j in range(SSM_TILES):
        uslab_ref[j] = proj_ref[:, UB0 + j * LANES:UB0 + (j + 1) * LANES]
    for j in range(SSM_TILES):
        u_perm = jnp.concatenate(
            [uslab_ref[j, pl.ds(s, SCAN_CHUNKS, stride=SCAN_LEN), :] for s in range(SCAN_LEN)], axis=0)
        hs_ref[:, j * SSM_TILE_W:(j + 1) * SSM_TILE_W] = _dot(u_perm.astype(BF16), bb_ref[j])

    for c0 in list(range(Q0, ZA0, PROJ_CT)) + list(range(UC0, ZC0, PROJ_CT)):
        _proj_chunk(xb, w_in_ref, proj_ref, c0)
    proj_ref[:, VC0:VC0 + BRANCH_W] = _sgu_layernorm(proj_ref[:, VC0:VC0 + BRANCH_W], ptab_ref)

    srow = lax.broadcasted_iota(jnp.int32, (SUBLANES, SSM_HALF), 0)
    half_len = SCAN_LEN // 2
    for j in range(SSM_TILES):
        re, im = _re(j), _im(j)
        a_r = atab_ref[0, :, re]
        a_i = atab_ref[0, :, im]
        project_late(2 if j < 2 else 1)
        c_r = jnp.broadcast_to(hcarry_ref[SUBLANES - 1:SUBLANES, re], (SUBLANES, SSM_HALF))
        c_i = jnp.broadcast_to(hcarry_ref[SUBLANES - 1:SUBLANES, im], (SUBLANES, SSM_HALF))
        h_r = hs_ref[0:SUBLANES, re] + jnp.where(srow == 0, a_r * c_r - a_i * c_i, 0.0)
        h_i = hs_ref[0:SUBLANES, im] + jnp.where(srow == 0, a_r * c_i + a_i * c_r, 0.0)
        hs_ref[0:SUBLANES, re] = h_r
        hs_ref[0:SUBLANES, im] = h_i
        for s in range(1, SCAN_LEN):
            if s == half_len:
                project_late(1)
            rows = slice(s * SUBLANES, (s + 1) * SUBLANES)
            h_r, h_i = (hs_ref[rows, re] + (a_r * h_r - a_i * h_i),
                        hs_ref[rows, im] + (a_r * h_i + a_i * h_r))
            hs_ref[rows, re] = h_r
            hs_ref[rows, im] = h_i
        e_r, e_i = h_r, h_i
        for k, d in enumerate((1, 2, 4)):
            if k == 0:
                p_r, p_i = atab_ref[SCAN_LEN - 1, :, re], atab_ref[SCAN_LEN - 1, :, im]
            else:
                p_r, p_i = apw_ref[k - 1, :, re], apw_ref[k - 1, :, im]
            s_r = pltpu.roll(e_r, d, 0)
            s_i = pltpu.roll(e_i, d, 0)
            e_r, e_i = (e_r + jnp.where(srow >= d, p_r * s_r - p_i * s_i, 0.0),
                        e_i + jnp.where(srow >= d, p_r * s_i + p_i * s_r, 0.0))
        hcarry_ref[:, re] = e_r
        hcarry_ref[:, im] = e_i
        hstate_ref[0, :, re] = e_r
        hstate_ref[0, :, im] = e_i
        in_r = jnp.where(srow == 0, 0.0, pltpu.roll(e_r, 1, 0))
        in_i = jnp.where(srow == 0, 0.0, pltpu.roll(e_i, 1, 0))
        for s in range(SCAN_LEN):
            if s % half_len == 0:
                project_late(1)
            rows = slice(s * SUBLANES, (s + 1) * SUBLANES)
            t_r = atab_ref[s, :, re]
            t_i = atab_ref[s, :, im]
            hs_ref[rows, re] = hs_ref[rows, re] + (t_r * in_r - t_i * in_i)
            hs_ref[rows, im] = hs_ref[rows, im] + (t_r * in_i + t_i * in_r)
    assert not late_chunks
    for j in range(SSM_TILES):
        y_perm = _dot(hs_ref[:, j * SSM_TILE_W:(j + 1) * SSM_TILE_W].astype(BF16), cc_ref[j])
        for s in range(SCAN_LEN):
            yslab_ref[j, pl.ds(s, SCAN_CHUNKS, stride=SCAN_LEN), :] = y_perm[s * SUBLANES:(s + 1) * SUBLANES, :]
    y_lin = jnp.concatenate([yslab_ref[j] for j in range(SSM_TILES)], axis=1)
    ybr_ref[:, BRANCH_W:2 * BRANCH_W] = _s5_output(y_lin, proj_ref[:, UB0:UB0 + BRANCH_W], ptab_ref, gluw_ref)

    kv_cur = proj_ref[:, K0:K0 + 2 * LANES]
    kv_prev = kvprev_ref[...]
    lane2 = lax.broadcasted_iota(jnp.int32, (2 * TB, LANES), 1)
    k_dup = _dup_halves(jnp.concatenate([kv_prev[:, 0:LANES], kv_cur[:, 0:LANES]], axis=0), lane2 < HEAD_DIM)
    v_dup = _dup_halves(jnp.concatenate([kv_prev[:, LANES:], kv_cur[:, LANES:]], axis=0), lane2 < HEAD_DIM)
    k_dup = [k.astype(BF16) for k in k_dup]
    v_dup = [v.astype(BF16) for v in v_dup]
    first_neg = jnp.where(ji == 0, NEG_INF, 0.0).astype(F32)
    kcol = lax.broadcasted_iota(jnp.int32, (TB, 2 * TB), 1)
    prev_mask = jnp.where(kcol < TB, first_neg, 0.0)
    lane = lax.broadcasted_iota(jnp.int32, (TB, LANES), 1)
    lo = lane < HEAD_DIM
    for t in range(N_HEADS // 2):
        qt = proj_ref[:, Q0 + t * LANES:Q0 + (t + 1) * LANES]
        kv_head = (2 * t) // GQA_GROUP
        halves = []
        for half in range(2):
            h = 2 * t + half
            qm = jnp.where(lo if half == 0 else jnp.logical_not(lo), qt, 0.0).astype(BF16)
            s = _dot_nt(qm, k_dup[kv_head]) + bias_ref[h] + prev_mask
            sink = sinks_ref[layer, h]
            m = jnp.maximum(jnp.max(s, axis=-1, keepdims=True), sink)
            p = jnp.exp(s - m)
            denom = jnp.sum(p, axis=-1, keepdims=True) + jnp.exp(sink - m)
            o = _dot(p.astype(BF16), v_dup[kv_head])
            halves.append(o * (1.0 / denom))
        ybr_ref[:, t * LANES:(t + 1) * LANES] = jnp.where(lo, halves[0], halves[1])
    kvprev_ref[...] = kv_cur
    kvwin_ref[0] = kv_cur

    trow = lax.broadcasted_iota(jnp.int32, (CHUNK, CHUNK), 0)
    tcol = lax.broadcasted_iota(jnp.int32, (CHUNK, CHUNK), 1)
    tril = trow >= tcol
    for g in range(SGU_GROUPS):
        wm = jnp.where(tril, sguw_ref[g], 0.0).astype(BF16)
        vn_g = proj_ref[:, VC0 + g * LANES:VC0 + (g + 1) * LANES]
        b_row = ptab_ref[ROW_SGU_B:ROW_SGU_B + 1, g * CHUNK:(g + 1) * CHUNK]
        b_col = jnp.sum(jnp.where(trow == tcol, b_row, 0.0), axis=1, keepdims=True)
        sg = _dot(wm, vn_g.astype(BF16)) + b_col
        ybr_ref[:, 2 * BRANCH_W + g * LANES:2 * BRANCH_W + (g + 1) * LANES] = (
            proj_ref[:, UC0 + g * LANES:UC0 + (g + 1) * LANES] * sg)

    y_ref[0] = _merge(x_ref[0], proj_ref, ybr_ref, wread_ref, wo_ref, ptab_ref)


def _layer_spec(layer, shape):
    nd = len(shape)
    return pl.BlockSpec((None,) + tuple(shape), lambda *_, _nd=nd: (layer,) + (0,) * _nd,
                        pipeline_mode=pl.Buffered(1))


def _prompt_layer(layer, x, sinks, lw):
    nb, t, _ = x.shape
    grid = (nb, t // TB)
    weights = (lw['w_in'], lw['a_row'], lw['bb'], lw['cc'], lw['ptab'], lw['glu_w'], lw['sgu_w'],
               lw['w_read'], lw['w_o'])
    in_specs = [pl.BlockSpec(memory_space=pltpu.SMEM),
                pl.BlockSpec((1, TB, D_MODEL), lambda b, j: (b, j, 0))]
    in_specs += [_layer_spec(layer, w.shape[1:]) for w in weights]
    out_shape = (jax.ShapeDtypeStruct((nb, t, D_MODEL), F32),
                 jax.ShapeDtypeStruct((nb, TB, 2 * LANES), F32),
                 jax.ShapeDtypeStruct((nb, SUBLANES, 2 * SSM_W), F32))
    out_specs = (pl.BlockSpec((1, TB, D_MODEL), lambda b, j: (b, j, 0)),
                 pl.BlockSpec((1, TB, 2 * LANES), lambda b, j: (b, 0, 0)),
                 pl.BlockSpec((1, SUBLANES, 2 * SSM_W), lambda b, j: (b, 0, 0)))
    scratch = [pltpu.VMEM((TB, N_IN), F32),
               pltpu.VMEM((TB, N_BRANCH * BRANCH_W), F32),
               pltpu.VMEM((TB, 2 * LANES), F32),
               pltpu.VMEM((N_HEADS, TB, 2 * TB), F32),
               pltpu.VMEM((SCAN_LEN, SUBLANES, 2 * SSM_W), F32),
               pltpu.VMEM((2, SUBLANES, 2 * SSM_W), F32),
               pltpu.VMEM((TB, 2 * SSM_W), F32),
               pltpu.VMEM((SUBLANES, 2 * SSM_W), F32),
               pltpu.VMEM((SSM_TILES, TB, LANES), F32),
               pltpu.VMEM((SSM_TILES, TB, LANES), F32)]
    return pl.pallas_call(
        functools.partial(_prompt_kernel, layer),
        grid=grid,
        in_specs=in_specs,
        out_specs=out_specs,
        out_shape=out_shape,
        scratch_shapes=scratch,
        compiler_params=pltpu.CompilerParams(
            dimension_semantics=("arbitrary", "arbitrary"),
            vmem_limit_bytes=VMEM_LIMIT_BYTES),
        name="prompt_layer",
    )(sinks, x, *weights)


def _sample_kernel(layer, sinks_ref, x_ref, ck_ref, cv_ref, h0_ref, w_in_ref, arow_ref, bb_ref, cc_ref,
                   ptab_ref, gluw_ref, sguw_ref, wread_ref, wo_ref,
                   y_ref, kv_ref, hout_ref, vn_ref,
                   proj_ref, ybr_ref):
    n_seq = x_ref.shape[0]
    win = ck_ref.shape[1]
    xb = x_ref[...].astype(BF16)
    for c0 in range(0, N_IN, PROJ_CT):
        _proj_chunk(xb, w_in_ref, proj_ref, c0)
    kv_ref[...] = proj_ref[:, K0:K0 + 2 * LANES]

    row = lax.broadcasted_iota(jnp.int32, (SUBLANES, LANES), 0)
    lane = lax.broadcasted_iota(jnp.int32, (SUBLANES, LANES), 1)
    lane_lo = lane < HEAD_DIM
    row_even = (row % 2) == 0
    sel = jnp.logical_or(jnp.logical_and(row_even, lane_lo),
                         jnp.logical_and(jnp.logical_not(row_even), jnp.logical_not(lane_lo)))
    kv0 = row < GQA_GROUP
    slope = jnp.exp2(-(row.astype(F32) + 1.0))
    bias = -slope * (win - lane).astype(F32)
    sink = jnp.zeros((SUBLANES, 1), F32)
    for h in range(N_HEADS):
        sink = jnp.where(row[:, 0:1] == h, sinks_ref[layer, h], sink)
    lane_win = lax.broadcasted_iota(jnp.int32, (win, LANES), 1) < HEAD_DIM

    def seq_group_step(gi, carry):
        r0 = pl.multiple_of(gi * SUBLANES, SUBLANES)
        q_rows = proj_ref[pl.ds(r0, SUBLANES), Q0:Q0 + BRANCH_W]
        k_rows = _dup_halves(proj_ref[pl.ds(r0, SUBLANES), K0:K0 + LANES], lane_lo)
        v_rows = _dup_halves(proj_ref[pl.ds(r0, SUBLANES), V0:V0 + LANES], lane_lo)
        outs = [[] for _ in range(N_HEADS // 2)]
        for r in range(SUBLANES):
            n = r0 + r
            q8 = jnp.zeros((SUBLANES, LANES), F32)
            for t in range(N_HEADS // 2):
                qt = jnp.broadcast_to(q_rows[r:r + 1, t * LANES:(t + 1) * LANES], (SUBLANES, LANES))
                q8 = jnp.where(jnp.logical_and(sel, row // 2 == t), qt, q8)
            k0, k1 = _dup_halves(ck_ref[n], lane_win)
            v0, v1 = _dup_halves(cv_ref[n], lane_win)
            k_both = jnp.concatenate([k0, k1], axis=0).astype(BF16)
            v_both = jnp.concatenate([v0, v1], axis=1).astype(BF16)
            k_own = jnp.where(kv0, k_rows[0][r:r + 1, :], k_rows[1][r:r + 1, :])
            v_own = jnp.where(kv0, v_rows[0][r:r + 1, :], v_rows[1][r:r + 1, :])
            s2 = _dot_nt(q8.astype(BF16), k_both)
            s = jnp.where(kv0, s2[:, 0:win], s2[:, win:]) + bias
            s_own = jnp.sum(q8 * k_own, axis=-1, keepdims=True)
            m = jnp.maximum(jnp.maximum(jnp.max(s, axis=-1, keepdims=True), s_own), sink)
            p = jnp.exp(s - m)
            p_own = jnp.exp(s_own - m)
            denom = jnp.sum(p, axis=-1, keepdims=True) + p_own + jnp.exp(sink - m)
            o2 = _dot(p.astype(BF16), v_both)
            o = (jnp.where(kv0, o2[:, 0:LANES], o2[:, LANES:]) + p_own * v_own) * (1.0 / denom)
            o = jnp.where(sel, o, 0.0)
            for t in range(N_HEADS // 2):
                outs[t].append(o[2 * t:2 * t + 1, :] + o[2 * t + 1:2 * t + 2, :])
        for t in range(N_HEADS // 2):
            ybr_ref[pl.ds(r0, SUBLANES), t * LANES:(t + 1) * LANES] = jnp.concatenate(outs[t], axis=0)
        return carry

    lax.fori_loop(0, n_seq // SUBLANES, seq_group_step, 0)

    vn = _sgu_layernorm(proj_ref[:, VC0:VC0 + BRANCH_W], ptab_ref)
    vn_ref[...] = vn
    for g in range(SGU_GROUPS):
        w00 = sguw_ref[g, 0:1, 0:1]
        b0 = ptab_ref[ROW_SGU_B:ROW_SGU_B + 1, g * CHUNK:g * CHUNK + 1]
        cols = slice(g * LANES, (g + 1) * LANES)
        ybr_ref[:, 2 * BRANCH_W + g * LANES:2 * BRANCH_W + (g + 1) * LANES] = (
            proj_ref[:, UC0 + g * LANES:UC0 + (g + 1) * LANES] * (w00 * vn[:, cols] + b0))

    u = proj_ref[:, UB0:UB0 + BRANCH_W]
    y_tiles = []
    for j in range(SSM_TILES):
        re, im = _re(j), _im(j)
        xs = _dot(u[:, j * LANES:(j + 1) * LANES].astype(BF16), bb_ref[j])
        a_r = arow_ref[:, re]
        a_i = arow_ref[:, im]
        h0r = h0_ref[:, re]
        h0i = h0_ref[:, im]
        hr = xs[:, 0:SSM_HALF] + (a_r * h0r - a_i * h0i)
        hi = xs[:, SSM_HALF:] + (a_r * h0i + a_i * h0r)
        hout_ref[:, re] = hr
        hout_ref[:, im] = hi
        y_tiles.append(_dot(jnp.concatenate([hr, hi], axis=1).astype(BF16), cc_ref[j]))
    y_lin = jnp.concatenate(y_tiles, axis=1)
    ybr_ref[:, BRANCH_W:2 * BRANCH_W] = _s5_output(y_lin, u, ptab_ref, gluw_ref)

    y_ref[...] = _merge(x_ref[...], proj_ref, ybr_ref, wread_ref, wo_ref, ptab_ref)


def _sample_layer(layer, x, sinks, ck, cv, h0, lw):
    n = x.shape[0]
    per_layer = (ck, cv, h0, lw['w_in'], lw['a_row'], lw['bb'], lw['cc'], lw['ptab'], lw['glu_w'],
                 lw['sgu_w'], lw['w_read'], lw['w_o'])
    in_specs = [pl.BlockSpec(memory_space=pltpu.SMEM),
                pl.BlockSpec(x.shape, lambda i: (0, 0), pipeline_mode=pl.Buffered(1))]
    in_specs += [_layer_spec(layer, a.shape[1:]) for a in per_layer]
    out_shape = (jax.ShapeDtypeStruct((n, D_MODEL), F32),
                 jax.ShapeDtypeStruct((n, 2 * LANES), F32),
                 jax.ShapeDtypeStruct((n, 2 * SSM_W), F32),
                 jax.ShapeDtypeStruct((n, BRANCH_W), F32))
    out_specs = tuple(pl.BlockSpec(s.shape, lambda i: (0, 0)) for s in out_shape)
    scratch = [pltpu.VMEM((n, N_IN), F32), pltpu.VMEM((n, N_BRANCH * BRANCH_W), F32)]
    return pl.pallas_call(
        functools.partial(_sample_kernel, layer),
        grid=(1,),
        in_specs=in_specs,
        out_specs=out_specs,
        out_shape=out_shape,
        scratch_shapes=scratch,
        compiler_params=pltpu.CompilerParams(
            dimension_semantics=("arbitrary",), vmem_limit_bytes=VMEM_LIMIT_BYTES),
        name="sample_layer",
    )(sinks, x, *per_layer)


def _param_table(ssm_d, glu_b, sgu_ln_g, sgu_ln_b, ln_g, ln_b, sgu_b):
    depth = ln_g.shape[0]
    rows = [jnp.concatenate([ssm_d, glu_b], axis=1),
            jnp.concatenate([sgu_ln_g, sgu_ln_b], axis=1),
            ln_g, ln_b,
            jnp.concatenate([sgu_b.reshape(depth, BRANCH_W), jnp.zeros((depth, BRANCH_W), F32)], axis=1)]
    rows += [jnp.zeros((depth, D_MODEL), F32)] * (PARAM_ROWS - len(rows))
    return jnp.stack(rows, axis=1)


def kernel(x_prompt, x_sample, cache_k_win, cache_v_win, state_ssm_re, state_ssm_im, w_in, attn_sinks,
           ssm_lambda_re, ssm_lambda_im, ssm_log_dt, ssm_b_re, ssm_b_im, ssm_c_re, ssm_c_im, ssm_d,
           glu_w, glu_b, sgu_ln_g, sgu_ln_b, sgu_w, sgu_b, w_read, w_o, ln_g, ln_b):
    nb, t, _ = x_prompt.shape
    ns = x_sample.shape[0]
    depth, _, win = cache_k_win.shape[:3]
    assert t % TB == 0 and win == WINDOW == TB and ns % SUBLANES == 0

    a_row, bb, cc = _ssm_prep(ssm_lambda_re, ssm_lambda_im, ssm_log_dt, ssm_b_re, ssm_b_im, ssm_c_re, ssm_c_im)
    lw = dict(w_in=w_in.astype(BF16), a_row=a_row, bb=bb, cc=cc,
              ptab=_param_table(ssm_d, glu_b, sgu_ln_g, sgu_ln_b, ln_g, ln_b, sgu_b),
              glu_w=glu_w.astype(BF16), sgu_w=sgu_w, w_read=w_read.astype(BF16), w_o=w_o.astype(BF16))
    ck = cache_k_win.reshape(depth, ns, win, KV_HEADS * HEAD_DIM)
    cv = cache_v_win.reshape(depth, ns, win, KV_HEADS * HEAD_DIM)
    h0 = _state_to_kernel_layout(state_ssm_re, state_ssm_im)

    xp = x_prompt
    xs = x_sample.reshape(ns, D_MODEL)
    kp, vp, hrp, hip = [], [], [], []
    ksm, vsm, hrs, his, vcs = [], [], [], [], []
    for l in range(depth):
        xp, kvwin, hstate = _prompt_layer(l, xp, attn_sinks, lw)
        kp.append(kvwin[:, :, 0:LANES].reshape(nb, TB, KV_HEADS, HEAD_DIM))
        vp.append(kvwin[:, :, LANES:].reshape(nb, TB, KV_HEADS, HEAD_DIM))
        h_re, h_im = _state_from_kernel_layout(hstate[:, SUBLANES - 1, :])
        hrp.append(h_re)
        hip.append(h_im)

        xs, kv_s, h_s, vn_s = _sample_layer(l, xs, attn_sinks, ck, cv, h0, lw)
        ksm.append(kv_s[:, 0:LANES].reshape(ns, 1, KV_HEADS, HEAD_DIM))
        vsm.append(kv_s[:, LANES:].reshape(ns, 1, KV_HEADS, HEAD_DIM))
        h_re, h_im = _state_from_kernel_layout(h_s)
        hrs.append(h_re)
        his.append(h_im)
        vcs.append(vn_s.reshape(ns, 1, BRANCH_W))
    return (xp, xs.reshape(ns, 1, D_MODEL), jnp.stack(kp), jnp.stack(vp), jnp.stack(hrp), jnp.stack(hip),
            jnp.stack(ksm), jnp.stack(vsm), jnp.stack(hrs), jnp.stack(his), jnp.stack(vcs))
```

```python
import functools
import math

import jax
import jax.numpy as jnp
from jax import lax
from jax.experimental import pallas as pl
from jax.experimental.pallas import tpu as pltpu

D_MODEL = 1024
BRANCH_W = 512
HEAD_DIM = 64
N_HEADS = 8
KV_HEADS = 2
GQA_GROUP = N_HEADS // KV_HEADS
WINDOW = 128
SSM_GROUP_CH = 16
SSM_GROUPS = 32
SSM_STATE = 64
SSM_W = SSM_GROUPS * SSM_STATE
CHUNK = 128
SGU_GROUPS = 4
N_BRANCH = 3
DEPTH = 2
ALPHA = (2 * DEPTH) ** 0.25
LN_EPS = 1e-5
ATTN_SCALE = HEAD_DIM ** -0.5
NEG_INF = -1e30

Q0, K0, V0, ZA0, UB0, ZB0, UC0, VC0, ZC0, G0 = 0, 512, 640, 768, 1280, 1792, 2304, 2816, 3328, 3840
N_IN = 6912

LANES = 128
SUBLANES = 8
MXU_COLS = 256
VMEM_LIMIT_BYTES = 60 * 1024 * 1024

SUB = 128
N_SUB = 2
TB = N_SUB * SUB
PROJ_CT = MXU_COLS

SSM_TILES = BRANCH_W // LANES
SSM_TILE_GROUPS = SSM_GROUPS // SSM_TILES
SSM_HALF = SSM_W // SSM_TILES
SSM_TILE_W = 2 * SSM_HALF
SCAN_CHUNKS = SUBLANES
SCAN_LEN = SUB // SCAN_CHUNKS

ROW_DSKIP_GLUB, ROW_SGU_LN, ROW_LN_G, ROW_LN_B, ROW_SGU_B = 0, 1, 2, 3, 4
PARAM_ROWS = SUBLANES

F32 = jnp.float32
BF16 = jnp.bfloat16


def _re(j):
    return slice(j * SSM_TILE_W, j * SSM_TILE_W + SSM_HALF)


def _im(j):
    return slice(j * SSM_TILE_W + SSM_HALF, (j + 1) * SSM_TILE_W)


def _sigmoid(x):
    return 1.0 / (1.0 + jnp.exp(-x))


def _silu(x):
    return x * _sigmoid(x)


def _gelu_tanh(x):
    c = math.sqrt(2.0 / math.pi)
    return 0.5 * x * (1.0 + jnp.tanh(c * (x + 0.044715 * (x * x * x))))


def _layernorm(x, g, b):
    mu = jnp.mean(x, axis=-1, keepdims=True)
    xc = x - mu
    var = jnp.mean(xc * xc, axis=-1, keepdims=True)
    return xc * lax.rsqrt(var + LN_EPS) * g + b


def _dot(a, b):
    return jnp.dot(a, b, preferred_element_type=F32)


def _dot_nt(a, b):
    return lax.dot_general(a, b, (((1,), (1,)), ((), ())), preferred_element_type=F32)


def _slope(h):
    return 2.0 ** (-(h + 1.0))


def _dup_halves(x, lo):
    sw = pltpu.roll(x, HEAD_DIM, 1)
    return jnp.where(lo, x, sw), jnp.where(lo, sw, x)


def _ssm_prep_kernel(lr_ref, li_ref, ldt_ref, brt_ref, bit_ref, crt_ref, cit_ref,
                     arow_ref, bb_ref, cc_ref, bbs_ref, ccs_ref):
    lr = lr_ref[...]
    li = li_ref[...]
    dt = jnp.exp(ldt_ref[...])
    mag = jnp.exp(lr * dt)
    ar = mag * jnp.cos(li * dt)
    ai = mag * jnp.sin(li * dt)
    den = lr * lr + li * li
    cr = ((ar - 1.0) * lr + ai * li) / den
    ci = (ai * lr - (ar - 1.0) * li) / den
    for j in range(SSM_TILES):
        bbs_ref[...] = jnp.zeros((LANES, SSM_TILE_W), F32)
        ccs_ref[...] = jnp.zeros((SSM_TILE_W, LANES), F32)
        for gl in range(SSM_TILE_GROUPS):
            g = j * SSM_TILE_GROUPS + gl
            cols = slice(gl * SSM_STATE, (gl + 1) * SSM_STATE)
            cols_im = slice(SSM_HALF + gl * SSM_STATE, SSM_HALF + (gl + 1) * SSM_STATE)
            ch = slice(gl * SSM_GROUP_CH, (gl + 1) * SSM_GROUP_CH)
            arow_ref[:, j * SSM_TILE_W + gl * SSM_STATE:j * SSM_TILE_W + (gl + 1) * SSM_STATE] = ar[g:g + 1, :]
            arow_ref[:, j * SSM_TILE_W + SSM_HALF + gl * SSM_STATE:
                     j * SSM_TILE_W + SSM_HALF + (gl + 1) * SSM_STATE] = ai[g:g + 1, :]
            br = brt_ref[g]
            bi = bit_ref[g]
            crg = cr[g:g + 1, :]
            cig = ci[g:g + 1, :]
            bbs_ref[ch, cols] = crg * br - cig * bi
            bbs_ref[ch, cols_im] = crg * bi + cig * br
            ccs_ref[cols, ch] = crt_ref[g]
            ccs_ref[cols_im, ch] = -cit_ref[g]
        bb_ref[j] = bbs_ref[...].astype(BF16)
        cc_ref[j] = ccs_ref[...].astype(BF16)


def _ssm_prep(lam_re, lam_im, log_dt, b_re, b_im, c_re, c_im):
    depth = lam_re.shape[0]
    brt = jnp.swapaxes(b_re, 2, 3)
    bit = jnp.swapaxes(b_im, 2, 3)
    crt = jnp.swapaxes(c_re, 2, 3)
    cit = jnp.swapaxes(c_im, 2, 3)

    def spec(*shape):
        nd = len(shape)
        return pl.BlockSpec((None,) + shape, lambda l, _nd=nd: (l,) + (0,) * _nd)

    return pl.pallas_call(
        _ssm_prep_kernel,
        grid=(depth,),
        in_specs=[spec(SSM_GROUPS, SSM_STATE), spec(SSM_GROUPS, SSM_STATE), spec(SSM_GROUPS, 1),
                  spec(SSM_GROUPS, SSM_GROUP_CH, SSM_STATE), spec(SSM_GROUPS, SSM_GROUP_CH, SSM_STATE),
                  spec(SSM_GROUPS, SSM_STATE, SSM_GROUP_CH), spec(SSM_GROUPS, SSM_STATE, SSM_GROUP_CH)],
        out_specs=(spec(1, 2 * SSM_W), spec(SSM_TILES, LANES, SSM_TILE_W), spec(SSM_TILES, SSM_TILE_W, LANES)),
        out_shape=(jax.ShapeDtypeStruct((depth, 1, 2 * SSM_W), F32),
                   jax.ShapeDtypeStruct((depth, SSM_TILES, LANES, SSM_TILE_W), BF16),
                   jax.ShapeDtypeStruct((depth, SSM_TILES, SSM_TILE_W, LANES), BF16)),
        scratch_shapes=[pltpu.VMEM((LANES, SSM_TILE_W), F32), pltpu.VMEM((SSM_TILE_W, LANES), F32)],
        compiler_params=pltpu.CompilerParams(dimension_semantics=("arbitrary",)),
        name="ssm_prep",
    )(lam_re, lam_im, log_dt.reshape(depth, SSM_GROUPS, 1), brt, bit, crt, cit)


def _state_to_kernel_layout(h_re, h_im):
    lead = h_re.shape[:-2]
    return jnp.stack([h_re.reshape(lead + (SSM_TILES, SSM_HALF)), h_im.reshape(lead + (SSM_TILES, SSM_HALF))],
                     axis=-2).reshape(lead + (2 * SSM_W,))


def _state_from_kernel_layout(h):
    n = h.shape[0]
    h4 = h.reshape(n, SSM_TILES, 2, SSM_HALF)
    return (h4[:, :, 0].reshape(n, SSM_GROUPS, SSM_STATE), h4[:, :, 1].reshape(n, SSM_GROUPS, SSM_STATE))


def _proj_chunk(xb, w_in_ref, proj_ref, c0):
    y = _dot(xb, w_in_ref[:, c0:c0 + PROJ_CT])
    if c0 >= G0:
        y = _sigmoid(y)
    elif any(z0 <= c0 < z0 + BRANCH_W for z0 in (ZA0, ZB0, ZC0)):
        y = _silu(y)
    elif c0 < K0:
        y = y * ATTN_SCALE
    proj_ref[:, c0:c0 + PROJ_CT] = y


def _s5_output(y_lin, u, ptab_ref, gluw_ref):
    y = y_lin + ptab_ref[ROW_DSKIP_GLUB:ROW_DSKIP_GLUB + 1, 0:BRANCH_W] * u
    y = _gelu_tanh(y)
    glu_b = ptab_ref[ROW_DSKIP_GLUB:ROW_DSKIP_GLUB + 1, BRANCH_W:2 * BRANCH_W]
    return y * _sigmoid(_dot(y.astype(BF16), gluw_ref[...]) + glu_b)


def _merge(x, proj_ref, ybr_ref, wread_ref, wo_ref, ptab_ref):
    merged = None
    for b, z0 in enumerate((ZA0, ZB0, ZC0)):
        yb = ybr_ref[:, b * BRANCH_W:(b + 1) * BRANCH_W] * proj_ref[:, z0:z0 + BRANCH_W]
        branch = _dot(yb.astype(BF16), wread_ref[b])
        term = proj_ref[:, G0 + b * D_MODEL:G0 + (b + 1) * D_MODEL] * branch
        merged = term if merged is None else merged + term
    out = _dot(merged.astype(BF16), wo_ref[...])
    return _layernorm(ALPHA * x + out, ptab_ref[ROW_LN_G:ROW_LN_G + 1, :], ptab_ref[ROW_LN_B:ROW_LN_B + 1, :])


def _sgu_layernorm(v, ptab_ref):
    return _layernorm(v, ptab_ref[ROW_SGU_LN:ROW_SGU_LN + 1, 0:BRANCH_W],
                      ptab_ref[ROW_SGU_LN:ROW_SGU_LN + 1, BRANCH_W:2 * BRANCH_W])


def _prompt_kernel(layer, sinks_ref, x_ref, w_in_ref, arow_ref, bb_ref, cc_ref, ptab_ref, gluw_ref,
                   sguw_ref, wread_ref, wo_ref,
                   y_ref, kvwin_ref, hstate_ref,
                   proj_ref, ybr_ref, kvprev_ref, bias_ref, atab_ref, apw_ref, hs_ref, hcarry_ref,
                   uslab_ref, yslab_ref):
    bi = pl.program_id(0)
    ji = pl.program_id(1)

    @pl.when(jnp.logical_and(bi == 0, ji == 0))
    def _init_tables():
        qa = lax.broadcasted_iota(jnp.int32, (SUB, 2 * SUB), 0)
        kk = lax.broadcasted_iota(jnp.int32, (SUB, 2 * SUB), 1)
        dist = qa + WINDOW - kk
        valid = jnp.logical_and(dist >= 0, dist <= WINDOW)
        distf = dist.astype(F32)
        for h in range(N_HEADS):
            bias_ref[h] = jnp.where(valid, -_slope(h) * distf, NEG_INF)
        for j in range(SSM_TILES):
            a_r = arow_ref[:, _re(j)]
            a_i = arow_ref[:, _im(j)]
            pr, pi = a_r, a_i
            for s in range(SCAN_LEN):
                atab_ref[s, :, _re(j)] = jnp.broadcast_to(pr, (SUBLANES, SSM_HALF))
                atab_ref[s, :, _im(j)] = jnp.broadcast_to(pi, (SUBLANES, SSM_HALF))
                if s + 1 < SCAN_LEN:
                    pr, pi = pr * a_r - pi * a_i, pr * a_i + pi * a_r
            for k in range(2):
                pr, pi = pr * pr - pi * pi, 2.0 * (pr * pi)
                apw_ref[k, :, _re(j)] = jnp.broadcast_to(pr, (SUBLANES, SSM_HALF))
                apw_ref[k, :, _im(j)] = jnp.broadcast_to(pi, (SUBLANES, SSM_HALF))

    @pl.when(ji == 0)
    def _reset_carries():
        kvprev_ref[...] = jnp.zeros((SUB, 2 * LANES), F32)
        hcarry_ref[...] = jnp.zeros((SUBLANES, 2 * SSM_W), F32)

    xb = x_ref[0].astype(BF16)
    late_chunks = ([z0 + c for z0 in (ZA0, ZB0, ZC0) for c in range(0, BRANCH_W, PROJ_CT)]
                   + list(range(G0, N_IN, PROJ_CT)))

    def project_late(n):
        for _ in range(n):
            _proj_chunk(xb, w_in_ref, proj_ref, late_chunks.pop(0))

    for c0 in range(UB0, UB0 + BRANCH_W, PROJ_CT):
        _proj_chunk(xb, w_in_ref, proj_ref, c0)

    for sb in range(N_SUB):
        for j in range(SSM_TILES):
            uslab_ref[sb * SSM_TILES + j] = proj_ref[sb * SUB:(sb + 1) * SUB, UB0 + j * LANES:UB0 + (j + 1) * LANES]
        for j in range(SSM_TILES):
            u_perm = jnp.concatenate(
                [uslab_ref[sb * SSM_TILES + j, pl.ds(s, SCAN_CHUNKS, stride=SCAN_LEN), :]
                 for s in range(SCAN_LEN)], axis=0)
            hs_ref[sb * SUB:(sb + 1) * SUB, j * SSM_TILE_W:(j + 1) * SSM_TILE_W] = _dot(
                u_perm.astype(BF16), bb_ref[j])

    for c0 in list(range(Q0, ZA0, PROJ_CT)) + list(range(UC0, ZC0, PROJ_CT)):
        _proj_chunk(xb, w_in_ref, proj_ref, c0)
    proj_ref[:, VC0:VC0 + BRANCH_W] = _sgu_layernorm(proj_ref[:, VC0:VC0 + BRANCH_W], ptab_ref)

    srow = lax.broadcasted_iota(jnp.int32, (SUBLANES, SSM_HALF), 0)
    half_len = SCAN_LEN // 2
    for sb, j in [(sb, j) for sb in range(N_SUB) for j in range(SSM_TILES)]:
        r_base = sb * SUB
        re, im = _re(j), _im(j)
        a_r = atab_ref[0, :, re]
        a_i = atab_ref[0, :, im]
        project_late(1)
        c_r = jnp.broadcast_to(hcarry_ref[SUBLANES - 1:SUBLANES, re], (SUBLANES, SSM_HALF))
        c_i = jnp.broadcast_to(hcarry_ref[SUBLANES - 1:SUBLANES, im], (SUBLANES, SSM_HALF))
        h_r = hs_ref[r_base:r_base + SUBLANES, re] + jnp.where(srow == 0, a_r * c_r - a_i * c_i, 0.0)
        h_i = hs_ref[r_base:r_base + SUBLANES, im] + jnp.where(srow == 0, a_r * c_i + a_i * c_r, 0.0)
        hs_ref[r_base:r_base + SUBLANES, re] = h_r
        hs_ref[r_base:r_base + SUBLANES, im] = h_i
        for s in range(1, SCAN_LEN):
            if s == half_len and sb == 0 and j < 2:
                project_late(1)
            rows = slice(r_base + s * SUBLANES, r_base + (s + 1) * SUBLANES)
            h_r, h_i = (hs_ref[rows, re] + (a_r * h_r - a_i * h_i),
                        hs_ref[rows, im] + (a_r * h_i + a_i * h_r))
            hs_ref[rows, re] = h_r
            hs_ref[rows, im] = h_i
        e_r, e_i = h_r, h_i
        for k, d in enumerate((1, 2, 4)):
            if k == 0:
                p_r, p_i = atab_ref[SCAN_LEN - 1, :, re], atab_ref[SCAN_LEN - 1, :, im]
            else:
                p_r, p_i = apw_ref[k - 1, :, re], apw_ref[k - 1, :, im]
            s_r = pltpu.roll(e_r, d, 0)
            s_i = pltpu.roll(e_i, d, 0)
            e_r, e_i = (e_r + jnp.where(srow >= d, p_r * s_r - p_i * s_i, 0.0),
                        e_i + jnp.where(srow >= d, p_r * s_i + p_i * s_r, 0.0))
        hcarry_ref[:, re] = e_r
        hcarry_ref[:, im] = e_i
        hstate_ref[0, :, re] = e_r
        hstate_ref[0, :, im] = e_i
        in_r = jnp.where(srow == 0, 0.0, pltpu.roll(e_r, 1, 0))
        in_i = jnp.where(srow == 0, 0.0, pltpu.roll(e_i, 1, 0))
        project_late(1)
        for s in range(SCAN_LEN):
            rows = slice(r_base + s * SUBLANES, r_base + (s + 1) * SUBLANES)
            t_r = atab_ref[s, :, re]
            t_i = atab_ref[s, :, im]
            hs_ref[rows, re] = hs_ref[rows, re] + (t_r * in_r - t_i * in_i)
            hs_ref[rows, im] = hs_ref[rows, im] + (t_r * in_i + t_i * in_r)
    assert not late_chunks
    for sb in range(N_SUB):
        for j in range(SSM_TILES):
            y_perm = _dot(hs_ref[sb * SUB:(sb + 1) * SUB, j * SSM_TILE_W:(j + 1) * SSM_TILE_W].astype(BF16),
                          cc_ref[j])
            for s in range(SCAN_LEN):
                yslab_ref[sb * SSM_TILES + j, pl.ds(s, SCAN_CHUNKS, stride=SCAN_LEN), :] = (
                    y_perm[s * SUBLANES:(s + 1) * SUBLANES, :])
    y_lin = jnp.concatenate(
        [jnp.concatenate([yslab_ref[sb * SSM_TILES + j] for j in range(SSM_TILES)], axis=1)
         for sb in range(N_SUB)], axis=0)
    ybr_ref[:, BRANCH_W:2 * BRANCH_W] = _s5_output(y_lin, proj_ref[:, UB0:UB0 + BRANCH_W], ptab_ref, gluw_ref)

    lane2 = lax.broadcasted_iota(jnp.int32, (2 * SUB, LANES), 1)
    kcol = lax.broadcasted_iota(jnp.int32, (SUB, 2 * SUB), 1)
    lane = lax.broadcasted_iota(jnp.int32, (SUB, LANES), 1)
    lo = lane < HEAD_DIM
    first_neg = jnp.where(ji == 0, NEG_INF, 0.0).astype(F32)
    kv_prev = kvprev_ref[...]
    for sb in range(N_SUB):
        rows = slice(sb * SUB, (sb + 1) * SUB)
        kv_cur = proj_ref[rows, K0:K0 + 2 * LANES]
        k_dup = _dup_halves(jnp.concatenate([kv_prev[:, 0:LANES], kv_cur[:, 0:LANES]], axis=0), lane2 < HEAD_DIM)
        v_dup = _dup_halves(jnp.concatenate([kv_prev[:, LANES:], kv_cur[:, LANES:]], axis=0), lane2 < HEAD_DIM)
        k_dup = [k.astype(BF16) for k in k_dup]
        v_dup = [v.astype(BF16) for v in v_dup]
        for t in range(N_HEADS // 2):
            qt = proj_ref[rows, Q0 + t * LANES:Q0 + (t + 1) * LANES]
            kv_head = (2 * t) // GQA_GROUP
            halves = []
            for half in range(2):
                h = 2 * t + half
                qm = jnp.where(lo if half == 0 else jnp.logical_not(lo), qt, 0.0).astype(BF16)
                s = _dot_nt(qm, k_dup[kv_head]) + bias_ref[h]
                if sb == 0:
                    s = s + jnp.where(kcol < SUB, first_neg, 0.0)
                sink = sinks_ref[layer, h]
                m = jnp.maximum(jnp.max(s, axis=-1, keepdims=True), sink)
                p = jnp.exp(s - m)
                denom = jnp.sum(p, axis=-1, keepdims=True) + jnp.exp(sink - m)
                o = _dot(p.astype(BF16), v_dup[kv_head])
                halves.append(o * (1.0 / denom))
            ybr_ref[rows, t * LANES:(t + 1) * LANES] = jnp.where(lo, halves[0], halves[1])
        kv_prev = kv_cur
    kvprev_ref[...] = kv_prev
    kvwin_ref[0] = kv_prev

    trow = lax.broadcasted_iota(jnp.int32, (CHUNK, CHUNK), 0)
    tcol = lax.broadcasted_iota(jnp.int32, (CHUNK, CHUNK), 1)
    tril = trow >= tcol
    for g in range(SGU_GROUPS):
        wm = jnp.where(tril, sguw_ref[g], 0.0).astype(BF16)
        b_row = ptab_ref[ROW_SGU_B:ROW_SGU_B + 1, g * CHUNK:(g + 1) * CHUNK]
        b_col = jnp.sum(jnp.where(trow == tcol, b_row, 0.0), axis=1, keepdims=True)
        for sb in range(N_SUB):
            rows = slice(sb * SUB, (sb + 1) * SUB)
            vn_g = proj_ref[rows, VC0 + g * LANES:VC0 + (g + 1) * LANES]
            sg = _dot(wm, vn_g.astype(BF16)) + b_col
            ybr_ref[rows, 2 * BRANCH_W + g * LANES:2 * BRANCH_W + (g + 1) * LANES] = (
                proj_ref[rows, UC0 + g * LANES:UC0 + (g + 1) * LANES] * sg)

    y_ref[0] = _merge(x_ref[0], proj_ref, ybr_ref, wread_ref, wo_ref, ptab_ref)


def _layer_spec(layer, shape):
    nd = len(shape)
    return pl.BlockSpec((None,) + tuple(shape), lambda *_, _nd=nd: (layer,) + (0,) * _nd,
                        pipeline_mode=pl.Buffered(1))


def _prompt_layer(layer, x, sinks, lw):
    nb, t, _ = x.shape
    grid = (nb, t // TB)
    weights = (lw['w_in'], lw['a_row'], lw['bb'], lw['cc'], lw['ptab'], lw['glu_w'], lw['sgu_w'],
               lw['w_read'], lw['w_o'])
    in_specs = [pl.BlockSpec(memory_space=pltpu.SMEM),
                pl.BlockSpec((1, TB, D_MODEL), lambda b, j: (b, j, 0))]
    in_specs += [_layer_spec(layer, w.shape[1:]) for w in weights]
    out_shape = (jax.ShapeDtypeStruct((nb, t, D_MODEL), F32),
                 jax.ShapeDtypeStruct((nb, SUB, 2 * LANES), F32),
                 jax.ShapeDtypeStruct((nb, SUBLANES, 2 * SSM_W), F32))
    out_specs = (pl.BlockSpec((1, TB, D_MODEL), lambda b, j: (b, j, 0)),
                 pl.BlockSpec((1, SUB, 2 * LANES), lambda b, j: (b, 0, 0)),
                 pl.BlockSpec((1, SUBLANES, 2 * SSM_W), lambda b, j: (b, 0, 0)))
    scratch = [pltpu.VMEM((TB, N_IN), F32),
               pltpu.VMEM((TB, N_BRANCH * BRANCH_W), F32),
               pltpu.VMEM((SUB, 2 * LANES), F32),
               pltpu.VMEM((N_HEADS, SUB, 2 * SUB), F32),
               pltpu.VMEM((SCAN_LEN, SUBLANES, 2 * SSM_W), F32),
               pltpu.VMEM((2, SUBLANES, 2 * SSM_W), F32),
               pltpu.VMEM((TB, 2 * SSM_W), F32),
               pltpu.VMEM((SUBLANES, 2 * SSM_W), F32),
               pltpu.VMEM((N_SUB * SSM_TILES, SUB, LANES), F32),
               pltpu.VMEM((N_SUB * SSM_TILES, SUB, LANES), F32)]
    return pl.pallas_call(
        functools.partial(_prompt_kernel, layer),
        grid=grid,
        in_specs=in_specs,
        out_specs=out_specs,
        out_shape=out_shape,
        scratch_shapes=scratch,
        compiler_params=pltpu.CompilerParams(
            dimension_semantics=("arbitrary", "arbitrary"),
            vmem_limit_bytes=VMEM_LIMIT_BYTES),
        name="prompt_layer",
    )(sinks, x, *weights)


def _sample_kernel(layer, sinks_ref, x_ref, ck_ref, cv_ref, h0_ref, w_in_ref, arow_ref, bb_ref, cc_ref,
                   ptab_ref, gluw_ref, sguw_ref, wread_ref, wo_ref,
                   y_ref, kv_ref, hout_ref, vn_ref,
                   proj_ref, ybr_ref):
    n_seq = x_ref.shape[0]
    win = ck_ref.shape[1]
    xb = x_ref[...].astype(BF16)
    for c0 in range(0, N_IN, PROJ_CT):
        _proj_chunk(xb, w_in_ref, proj_ref, c0)
    kv_ref[...] = proj_ref[:, K0:K0 + 2 * LANES]

    row = lax.broadcasted_iota(jnp.int32, (SUBLANES, LANES), 0)
    lane = lax.broadcasted_iota(jnp.int32, (SUBLANES, LANES), 1)
    lane_lo = lane < HEAD_DIM
    row_even = (row % 2) == 0
    sel = jnp.logical_or(jnp.logical_and(row_even, lane_lo),
                         jnp.logical_and(jnp.logical_not(row_even), jnp.logical_not(lane_lo)))
    kv0 = row < GQA_GROUP
    slope = jnp.exp2(-(row.astype(F32) + 1.0))
    bias = -slope * (win - lane).astype(F32)
    sink = jnp.zeros((SUBLANES, 1), F32)
    for h in range(N_HEADS):
        sink = jnp.where(row[:, 0:1] == h, sinks_ref[layer, h], sink)
    lane_win = lax.broadcasted_iota(jnp.int32, (win, LANES), 1) < HEAD_DIM

    def seq_group_step(gi, carry):
        r0 = pl.multiple_of(gi * SUBLANES, SUBLANES)
        q_rows = proj_ref[pl.ds(r0, SUBLANES), Q0:Q0 + BRANCH_W]
        k_rows = _dup_halves(proj_ref[pl.ds(r0, SUBLANES), K0:K0 + LANES], lane_lo)
        v_rows = _dup_halves(proj_ref[pl.ds(r0, SUBLANES), V0:V0 + LANES], lane_lo)
        outs = [[] for _ in range(N_HEADS // 2)]
        for r in range(SUBLANES):
            n = r0 + r
            q8 = jnp.zeros((SUBLANES, LANES), F32)
            for t in range(N_HEADS // 2):
                qt = jnp.broadcast_to(q_rows[r:r + 1, t * LANES:(t + 1) * LANES], (SUBLANES, LANES))
                q8 = jnp.where(jnp.logical_and(sel, row // 2 == t), qt, q8)
            k0, k1 = _dup_halves(ck_ref[n], lane_win)
            v0, v1 = _dup_halves(cv_ref[n], lane_win)
            k_both = jnp.concatenate([k0, k1], axis=0).astype(BF16)
            v_both = jnp.concatenate([v0, v1], axis=1).astype(BF16)
            k_own = jnp.where(kv0, k_rows[0][r:r + 1, :], k_rows[1][r:r + 1, :])
            v_own = jnp.where(kv0, v_rows[0][r:r + 1, :], v_rows[1][r:r + 1, :])
            s2 = _dot_nt(q8.astype(BF16), k_both)
            s = jnp.where(kv0, s2[:, 0:win], s2[:, win:]) + bias
            s_own = jnp.sum(q8 * k_own, axis=-1, keepdims=True)
            m = jnp.maximum(jnp.maximum(jnp.max(s, axis=-1, keepdims=True), s_own), sink)
            p = jnp.exp(s - m)
            p_own = jnp.exp(s_own - m)
            denom = jnp.sum(p, axis=-1, keepdims=True) + p_own + jnp.exp(sink - m)
            o2 = _dot(p.astype(BF16), v_both)
            o = (jnp.where(kv0, o2[:, 0:LANES], o2[:, LANES:]) + p_own * v_own) * (1.0 / denom)
            o = jnp.where(sel, o, 0.0)
            for t in range(N_HEADS // 2):
                outs[t].append(o[2 * t:2 * t + 1, :] + o[2 * t + 1:2 * t + 2, :])
        for t in range(N_HEADS // 2):
            ybr_ref[pl.ds(r0, SUBLANES), t * LANES:(t + 1) * LANES] = jnp.concatenate(outs[t], axis=0)
        return carry

    lax.fori_loop(0, n_seq // SUBLANES, seq_group_step, 0)

    vn = _sgu_layernorm(proj_ref[:, VC0:VC0 + BRANCH_W], ptab_ref)
    vn_ref[...] = vn
    for g in range(SGU_GROUPS):
        w00 = sguw_ref[g, 0:1, 0:1]
        b0 = ptab_ref[ROW_SGU_B:ROW_SGU_B + 1, g * CHUNK:g * CHUNK + 1]
        cols = slice(g * LANES, (g + 1) * LANES)
        ybr_ref[:, 2 * BRANCH_W + g * LANES:2 * BRANCH_W + (g + 1) * LANES] = (
            proj_ref[:, UC0 + g * LANES:UC0 + (g + 1) * LANES] * (w00 * vn[:, cols] + b0))

    u = proj_ref[:, UB0:UB0 + BRANCH_W]
    y_tiles = []
    for j in range(SSM_TILES):
        re, im = _re(j), _im(j)
        xs = _dot(u[:, j * LANES:(j + 1) * LANES].astype(BF16), bb_ref[j])
        a_r = arow_ref[:, re]
        a_i = arow_ref[:, im]
        h0r = h0_ref[:, re]
        h0i = h0_ref[:, im]
        hr = xs[:, 0:SSM_HALF] + (a_r * h0r - a_i * h0i)
        hi = xs[:, SSM_HALF:] + (a_r * h0i + a_i * h0r)
        hout_ref[:, re] = hr
        hout_ref[:, im] = hi
        y_tiles.append(_dot(jnp.concatenate([hr, hi], axis=1).astype(BF16), cc_ref[j]))
    y_lin = jnp.concatenate(y_tiles, axis=1)
    ybr_ref[:, BRANCH_W:2 * BRANCH_W] = _s5_output(y_lin, u, ptab_ref, gluw_ref)

    y_ref[...] = _merge(x_ref[...], proj_ref, ybr_ref, wread_ref, wo_ref, ptab_ref)


def _sample_layer(layer, x, sinks, ck, cv, h0, lw):
    n = x.shape[0]
    per_layer = (ck, cv, h0, lw['w_in'], lw['a_row'], lw['bb'], lw['cc'], lw['ptab'], lw['glu_w'],
                 lw['sgu_w'], lw['w_read'], lw['w_o'])
    in_specs = [pl.BlockSpec(memory_space=pltpu.SMEM),
                pl.BlockSpec(x.shape, lambda i: (0, 0), pipeline_mode=pl.Buffered(1))]
    in_specs += [_layer_spec(layer, a.shape[1:]) for a in per_layer]
    out_shape = (jax.ShapeDtypeStruct((n, D_MODEL), F32),
                 jax.ShapeDtypeStruct((n, 2 * LANES), F32),
                 jax.ShapeDtypeStruct((n, 2 * SSM_W), F32),
                 jax.ShapeDtypeStruct((n, BRANCH_W), F32))
    out_specs = tuple(pl.BlockSpec(s.shape, lambda i: (0, 0)) for s in out_shape)
    scratch = [pltpu.VMEM((n, N_IN), F32), pltpu.VMEM((n, N_BRANCH * BRANCH_W), F32)]
    return pl.pallas_call(
        functools.partial(_sample_kernel, layer),
        grid=(1,),
        in_specs=in_specs,
        out_specs=out_specs,
        out_shape=out_shape,
        scratch_shapes=scratch,
        compiler_params=pltpu.CompilerParams(
            dimension_semantics=("arbitrary",), vmem_limit_bytes=VMEM_LIMIT_BYTES),
        name="sample_layer",
    )(sinks, x, *per_layer)


def _param_table(ssm_d, glu_b, sgu_ln_g, sgu_ln_b, ln_g, ln_b, sgu_b):
    depth = ln_g.shape[0]
    rows = [jnp.concatenate([ssm_d, glu_b], axis=1),
            jnp.concatenate([sgu_ln_g, sgu_ln_b], axis=1),
            ln_g, ln_b,
            jnp.concatenate([sgu_b.reshape(depth, BRANCH_W), jnp.zeros((depth, BRANCH_W), F32)], axis=1)]
    rows += [jnp.zeros((depth, D_MODEL), F32)] * (PARAM_ROWS - len(rows))
    return jnp.stack(rows, axis=1)


def kernel(x_prompt, x_sample, cache_k_win, cache_v_win, state_ssm_re, state_ssm_im, w_in, attn_sinks,
           ssm_lambda_re, ssm_lambda_im, ssm_log_dt, ssm_b_re, ssm_b_im, ssm_c_re, ssm_c_im, ssm_d,
           glu_w, glu_b, sgu_ln_g, sgu_ln_b, sgu_w, sgu_b, w_read, w_o, ln_g, ln_b):
    nb, t, _ = x_prompt.shape
    ns = x_sample.shape[0]
    depth, _, win = cache_k_win.shape[:3]
    assert t % TB == 0 and win == WINDOW == SUB and ns % SUBLANES == 0

    a_row, bb, cc = _ssm_prep(ssm_lambda_re, ssm_lambda_im, ssm_log_dt, ssm_b_re, ssm_b_im, ssm_c_re, ssm_c_im)
    lw = dict(w_in=w_in.astype(BF16), a_row=a_row, bb=bb, cc=cc,
              ptab=_param_table(ssm_d, glu_b, sgu_ln_g, sgu_ln_b, ln_g, ln_b, sgu_b),
              glu_w=glu_w.astype(BF16), sgu_w=sgu_w, w_read=w_read.astype(BF16), w_o=w_o.astype(BF16))
    ck = cache_k_win.reshape(depth, ns, win, KV_HEADS * HEAD_DIM)
    cv = cache_v_win.reshape(depth, ns, win, KV_HEADS * HEAD_DIM)
    h0 = _state_to_kernel_layout(state_ssm_re, state_ssm_im)

    xp = x_prompt
    xs = x_sample.reshape(ns, D_MODEL)
    kp, vp, hrp, hip = [], [], [], []
    ksm, vsm, hrs, his, vcs = [], [], [], [], []
    for l in range(depth):
        xp, kvwin, hstate = _prompt_layer(l, xp, attn_sinks, lw)
        kp.append(kvwin[:, :, 0:LANES].reshape(nb, SUB, KV_HEADS, HEAD_DIM))
        vp.append(kvwin[:, :, LANES:].reshape(nb, SUB, KV_HEADS, HEAD_DIM))
        h_re, h_im = _state_from_kernel_layout(hstate[:, SUBLANES - 1, :])
        hrp.append(h_re)
        hip.append(h_im)

        xs, kv_s, h_s, vn_s = _sample_layer(l, xs, attn_sinks, ck, cv, h0, lw)
        ksm.append(kv_s[:, 0:LANES].reshape(ns, 1, KV_HEADS, HEAD_DIM))
        vsm.append(kv_s[:, LANES:].reshape(ns, 1, KV_HEADS, HEAD_DIM))
        h_re, h_im = _state_from_kernel_layout(h_s)
        hrs.append(h_re)
        his.append(h_im)
        vcs.append(vn_s.reshape(ns, 1, BRANCH_W))
    return (xp, xs.reshape(ns, 1, D_MODEL), jnp.stack(kp), jnp.stack(vp), jnp.stack(hrp), jnp.stack(hip),
            jnp.stack(ksm), jnp.stack(vsm), jnp.stack(hrs), jnp.stack(his), jnp.stack(vcs))
```

```python
import functools
import math

import jax
import jax.numpy as jnp
from jax import lax
from jax.experimental import pallas as pl
from jax.experimental.pallas import tpu as pltpu

D_MODEL = 1024
BRANCH_W = 512
HEAD_DIM = 64
N_HEADS = 8
KV_HEADS = 2
GQA_GROUP = N_HEADS // KV_HEADS
WINDOW = 128
SSM_GROUP_CH = 16
SSM_GROUPS = 32
SSM_STATE = 64
SSM_W = SSM_GROUPS * SSM_STATE
CHUNK = 128
SGU_GROUPS = 4
N_BRANCH = 3
DEPTH = 2
ALPHA = (2 * DEPTH) ** 0.25
LN_EPS = 1e-5
ATTN_SCALE = HEAD_DIM ** -0.5
NEG_INF = -1e30

Q0, K0, V0, ZA0, UB0, ZB0, UC0, VC0, ZC0, G0 = 0, 512, 640, 768, 1280, 1792, 2304, 2816, 3328, 3840
N_IN = 6912

LANES = 128
SUBLANES = 8
MXU_COLS = 256
VMEM_LIMIT_BYTES = 60 * 1024 * 1024

SUB = 128
N_SUB = 2
TB = N_SUB * SUB
PROJ_CT = MXU_COLS

SSM_TILES = BRANCH_W // LANES
SSM_TILE_GROUPS = SSM_GROUPS // SSM_TILES
SSM_HALF = SSM_W // SSM_TILES
SSM_TILE_W = 2 * SSM_HALF
SCAN_CHUNKS = SUBLANES
SCAN_LEN = SUB // SCAN_CHUNKS

ROW_DSKIP_GLUB, ROW_SGU_LN, ROW_LN_G, ROW_LN_B, ROW_SGU_B = 0, 1, 2, 3, 4
PARAM_ROWS = SUBLANES

F32 = jnp.float32
BF16 = jnp.bfloat16


def _re(j):
    return slice(j * SSM_TILE_W, j * SSM_TILE_W + SSM_HALF)


def _im(j):
    return slice(j * SSM_TILE_W + SSM_HALF, (j + 1) * SSM_TILE_W)


def _sigmoid(x):
    return 1.0 / (1.0 + jnp.exp(-x))


def _silu(x):
    return x * _sigmoid(x)


def _gelu_tanh(x):
    c = math.sqrt(2.0 / math.pi)
    return 0.5 * x * (1.0 + jnp.tanh(c * (x + 0.044715 * (x * x * x))))


def _layernorm(x, g, b):
    mu = jnp.mean(x, axis=-1, keepdims=True)
    xc = x - mu
    var = jnp.mean(xc * xc, axis=-1, keepdims=True)
    return xc * lax.rsqrt(var + LN_EPS) * g + b


def _dot(a, b):
    return jnp.dot(a, b, preferred_element_type=F32)


def _dot_nt(a, b):
    return lax.dot_general(a, b, (((1,), (1,)), ((), ())), preferred_element_type=F32)


def _slope(h):
    return 2.0 ** (-(h + 1.0))


def _dup_halves(x, lo):
    sw = pltpu.roll(x, HEAD_DIM, 1)
    return jnp.where(lo, x, sw), jnp.where(lo, sw, x)


def _ssm_prep_kernel(lr_ref, li_ref, ldt_ref, brt_ref, bit_ref, crt_ref, cit_ref,
                     arow_ref, bb_ref, cc_ref, bbs_ref, ccs_ref):
    lr = lr_ref[...]
    li = li_ref[...]
    dt = jnp.exp(ldt_ref[...])
    mag = jnp.exp(lr * dt)
    ar = mag * jnp.cos(li * dt)
    ai = mag * jnp.sin(li * dt)
    den = lr * lr + li * li
    cr = ((ar - 1.0) * lr + ai * li) / den
    ci = (ai * lr - (ar - 1.0) * li) / den
    for j in range(SSM_TILES):
        bbs_ref[...] = jnp.zeros((LANES, SSM_TILE_W), F32)
        ccs_ref[...] = jnp.zeros((SSM_TILE_W, LANES), F32)
        for gl in range(SSM_TILE_GROUPS):
            g = j * SSM_TILE_GROUPS + gl
            cols = slice(gl * SSM_STATE, (gl + 1) * SSM_STATE)
            cols_im = slice(SSM_HALF + gl * SSM_STATE, SSM_HALF + (gl + 1) * SSM_STATE)
            ch = slice(gl * SSM_GROUP_CH, (gl + 1) * SSM_GROUP_CH)
            arow_ref[:, j * SSM_TILE_W + gl * SSM_STATE:j * SSM_TILE_W + (gl + 1) * SSM_STATE] = ar[g:g + 1, :]
            arow_ref[:, j * SSM_TILE_W + SSM_HALF + gl * SSM_STATE:
                     j * SSM_TILE_W + SSM_HALF + (gl + 1) * SSM_STATE] = ai[g:g + 1, :]
            br = brt_ref[g]
            bi = bit_ref[g]
            crg = cr[g:g + 1, :]
            cig = ci[g:g + 1, :]
            bbs_ref[ch, cols] = crg * br - cig * bi
            bbs_ref[ch, cols_im] = crg * bi + cig * br
            ccs_ref[cols, ch] = crt_ref[g]
            ccs_ref[cols_im, ch] = -cit_ref[g]
        bb_ref[j] = bbs_ref[...].astype(BF16)
        cc_ref[j] = ccs_ref[...].astype(BF16)


def _ssm_prep(lam_re, lam_im, log_dt, b_re, b_im, c_re, c_im):
    depth = lam_re.shape[0]
    brt = jnp.swapaxes(b_re, 2, 3)
    bit = jnp.swapaxes(b_im, 2, 3)
    crt = jnp.swapaxes(c_re, 2, 3)
    cit = jnp.swapaxes(c_im, 2, 3)

    def spec(*shape):
        nd = len(shape)
        return pl.BlockSpec((None,) + shape, lambda l, _nd=nd: (l,) + (0,) * _nd)

    return pl.pallas_call(
        _ssm_prep_kernel,
        grid=(depth,),
        in_specs=[spec(SSM_GROUPS, SSM_STATE), spec(SSM_GROUPS, SSM_STATE), spec(SSM_GROUPS, 1),
                  spec(SSM_GROUPS, SSM_GROUP_CH, SSM_STATE), spec(SSM_GROUPS, SSM_GROUP_CH, SSM_STATE),
                  spec(SSM_GROUPS, SSM_STATE, SSM_GROUP_CH), spec(SSM_GROUPS, SSM_STATE, SSM_GROUP_CH)],
        out_specs=(spec(1, 2 * SSM_W), spec(SSM_TILES, LANES, SSM_TILE_W), spec(SSM_TILES, SSM_TILE_W, LANES)),
        out_shape=(jax.ShapeDtypeStruct((depth, 1, 2 * SSM_W), F32),
                   jax.ShapeDtypeStruct((depth, SSM_TILES, LANES, SSM_TILE_W), BF16),
                   jax.ShapeDtypeStruct((depth, SSM_TILES, SSM_TILE_W, LANES), BF16)),
        scratch_shapes=[pltpu.VMEM((LANES, SSM_TILE_W), F32), pltpu.VMEM((SSM_TILE_W, LANES), F32)],
        compiler_params=pltpu.CompilerParams(dimension_semantics=("arbitrary",)),
        name="ssm_prep",
    )(lam_re, lam_im, log_dt.reshape(depth, SSM_GROUPS, 1), brt, bit, crt, cit)


def _state_to_kernel_layout(h_re, h_im):
    lead = h_re.shape[:-2]
    return jnp.stack([h_re.reshape(lead + (SSM_TILES, SSM_HALF)), h_im.reshape(lead + (SSM_TILES, SSM_HALF))],
                     axis=-2).reshape(lead + (2 * SSM_W,))


def _state_from_kernel_layout(h):
    n = h.shape[0]
    h4 = h.reshape(n, SSM_TILES, 2, SSM_HALF)
    return (h4[:, :, 0].reshape(n, SSM_GROUPS, SSM_STATE), h4[:, :, 1].reshape(n, SSM_GROUPS, SSM_STATE))


def _proj_chunk(xb, w_in_ref, proj_ref, c0):
    y = _dot(xb, w_in_ref[:, c0:c0 + PROJ_CT])
    if c0 >= G0:
        y = _sigmoid(y)
    elif any(z0 <= c0 < z0 + BRANCH_W for z0 in (ZA0, ZB0, ZC0)):
        y = _silu(y)
    elif c0 < K0:
        y = y * ATTN_SCALE
    proj_ref[:, c0:c0 + PROJ_CT] = y


def _s5_output(y_lin, u, ptab_ref, gluw_ref):
    y = y_lin + ptab_ref[ROW_DSKIP_GLUB:ROW_DSKIP_GLUB + 1, 0:BRANCH_W] * u
    y = _gelu_tanh(y)
    glu_b = ptab_ref[ROW_DSKIP_GLUB:ROW_DSKIP_GLUB + 1, BRANCH_W:2 * BRANCH_W]
    return y * _sigmoid(_dot(y.astype(BF16), gluw_ref[...]) + glu_b)


def _gate_chunk(xb, w_in_ref, g0):
    return _sigmoid(_dot(xb, w_in_ref[:, g0:g0 + PROJ_CT]))


def _merge(x, w_in_ref, proj_ref, ybr_ref, merged_ref, wread_ref, wo_ref, ptab_ref, gates_ref=None):
    xb = x.astype(BF16)
    ybs = [(ybr_ref[:, b * BRANCH_W:(b + 1) * BRANCH_W] * proj_ref[:, z0:z0 + BRANCH_W]).astype(BF16)
           for b, z0 in enumerate((ZA0, ZB0, ZC0))]
    for n0 in range(0, D_MODEL, PROJ_CT):
        merged = None
        for b in range(N_BRANCH):
            g0 = b * D_MODEL + n0
            gate = (_gate_chunk(xb, w_in_ref, G0 + g0) if gates_ref is None
                    else gates_ref[:, g0:g0 + PROJ_CT])
            term = gate * _dot(ybs[b], wread_ref[b, :, n0:n0 + PROJ_CT])
            merged = term if merged is None else merged + term
        merged_ref[:, n0:n0 + PROJ_CT] = merged.astype(BF16)
    out = _dot(merged_ref[...], wo_ref[...])
    return _layernorm(ALPHA * x + out, ptab_ref[ROW_LN_G:ROW_LN_G + 1, :], ptab_ref[ROW_LN_B:ROW_LN_B + 1, :])


def _sgu_layernorm(v, ptab_ref):
    return _layernorm(v, ptab_ref[ROW_SGU_LN:ROW_SGU_LN + 1, 0:BRANCH_W],
                      ptab_ref[ROW_SGU_LN:ROW_SGU_LN + 1, BRANCH_W:2 * BRANCH_W])


def _prompt_kernel(layer, sinks_ref, x_ref, w_in_ref, arow_ref, bb_ref, cc_ref, ptab_ref, gluw_ref,
                   sguw_ref, wread_ref, wo_ref,
                   y_ref, kvwin_ref, hstate_ref,
                   proj_ref, ybr_ref, kvprev_ref, bias_ref, atab_ref, apw_ref, hs_ref, hcarry_ref,
                   uslab_ref, yslab_ref, merged_ref, gates_ref):
    bi = pl.program_id(0)
    ji = pl.program_id(1)

    @pl.when(jnp.logical_and(bi == 0, ji == 0))
    def _init_tables():
        qa = lax.broadcasted_iota(jnp.int32, (SUB, 2 * SUB), 0)
        kk = lax.broadcasted_iota(jnp.int32, (SUB, 2 * SUB), 1)
        dist = qa + WINDOW - kk
        valid = jnp.logical_and(dist >= 0, dist <= WINDOW)
        distf = dist.astype(F32)
        for h in range(N_HEADS):
            bias_ref[h] = jnp.where(valid, -_slope(h) * distf, NEG_INF)
        for j in range(SSM_TILES):
            a_r = arow_ref[:, _re(j)]
            a_i = arow_ref[:, _im(j)]
            pr, pi = a_r, a_i
            for s in range(SCAN_LEN):
                atab_ref[s, :, _re(j)] = jnp.broadcast_to(pr, (SUBLANES, SSM_HALF))
                atab_ref[s, :, _im(j)] = jnp.broadcast_to(pi, (SUBLANES, SSM_HALF))
                if s + 1 < SCAN_LEN:
                    pr, pi = pr * a_r - pi * a_i, pr * a_i + pi * a_r
            for k in range(2):
                pr, pi = pr * pr - pi * pi, 2.0 * (pr * pi)
                apw_ref[k, :, _re(j)] = jnp.broadcast_to(pr, (SUBLANES, SSM_HALF))
                apw_ref[k, :, _im(j)] = jnp.broadcast_to(pi, (SUBLANES, SSM_HALF))

    @pl.when(ji == 0)
    def _reset_carries():
        kvprev_ref[...] = jnp.zeros((SUB, 2 * LANES), F32)
        hcarry_ref[...] = jnp.zeros((SUBLANES, 2 * SSM_W), F32)

    xb = x_ref[0].astype(BF16)
    late_chunks = [z0 + c for z0 in (ZA0, ZB0, ZC0) for c in range(0, BRANCH_W, PROJ_CT)]

    def project_late(n):
        for _ in range(n):
            if late_chunks:
                _proj_chunk(xb, w_in_ref, proj_ref, late_chunks.pop(0))

    for c0 in range(UB0, UB0 + BRANCH_W, PROJ_CT):
        _proj_chunk(xb, w_in_ref, proj_ref, c0)

    for sb in range(N_SUB):
        for j in range(SSM_TILES):
            uslab_ref[sb * SSM_TILES + j] = proj_ref[sb * SUB:(sb + 1) * SUB, UB0 + j * LANES:UB0 + (j + 1) * LANES]
        for j in range(SSM_TILES):
            u_perm = jnp.concatenate(
                [uslab_ref[sb * SSM_TILES + j, pl.ds(s, SCAN_CHUNKS, stride=SCAN_LEN), :]
                 for s in range(SCAN_LEN)], axis=0)
            hs_ref[sb * SUB:(sb + 1) * SUB, j * SSM_TILE_W:(j + 1) * SSM_TILE_W] = _dot(
                u_perm.astype(BF16), bb_ref[j])

    for c0 in list(range(Q0, ZA0, PROJ_CT)) + list(range(UC0, ZC0, PROJ_CT)):
        _proj_chunk(xb, w_in_ref, proj_ref, c0)
    proj_ref[:, VC0:VC0 + BRANCH_W] = _sgu_layernorm(proj_ref[:, VC0:VC0 + BRANCH_W], ptab_ref)

    srow = lax.broadcasted_iota(jnp.int32, (SUBLANES, SSM_HALF), 0)
    for sb, j in [(sb, j) for sb in range(N_SUB) for j in range(SSM_TILES)]:
        r_base = sb * SUB
        re, im = _re(j), _im(j)
        a_r = atab_ref[0, :, re]
        a_i = atab_ref[0, :, im]
        project_late(1)
        c_r = jnp.broadcast_to(hcarry_ref[SUBLANES - 1:SUBLANES, re], (SUBLANES, SSM_HALF))
        c_i = jnp.broadcast_to(hcarry_ref[SUBLANES - 1:SUBLANES, im], (SUBLANES, SSM_HALF))
        h_r = hs_ref[r_base:r_base + SUBLANES, re] + jnp.where(srow == 0, a_r * c_r - a_i * c_i, 0.0)
        h_i = hs_ref[r_base:r_base + SUBLANES, im] + jnp.where(srow == 0, a_r * c_i + a_i * c_r, 0.0)
        hs_ref[r_base:r_base + SUBLANES, re] = h_r
        hs_ref[r_base:r_base + SUBLANES, im] = h_i
        for s in range(1, SCAN_LEN):
            rows = slice(r_base + s * SUBLANES, r_base + (s + 1) * SUBLANES)
            h_r, h_i = (hs_ref[rows, re] + (a_r * h_r - a_i * h_i),
                        hs_ref[rows, im] + (a_r * h_i + a_i * h_r))
            hs_ref[rows, re] = h_r
            hs_ref[rows, im] = h_i
        e_r, e_i = h_r, h_i
        for k, d in enumerate((1, 2, 4)):
            if k == 0:
                p_r, p_i = atab_ref[SCAN_LEN - 1, :, re], atab_ref[SCAN_LEN - 1, :, im]
            else:
                p_r, p_i = apw_ref[k - 1, :, re], apw_ref[k - 1, :, im]
            s_r = pltpu.roll(e_r, d, 0)
            s_i = pltpu.roll(e_i, d, 0)
            e_r, e_i = (e_r + jnp.where(srow >= d, p_r * s_r - p_i * s_i, 0.0),
                        e_i + jnp.where(srow >= d, p_r * s_i + p_i * s_r, 0.0))
        hcarry_ref[:, re] = e_r
        hcarry_ref[:, im] = e_i
        hstate_ref[0, :, re] = e_r
        hstate_ref[0, :, im] = e_i
        in_r = jnp.where(srow == 0, 0.0, pltpu.roll(e_r, 1, 0))
        in_i = jnp.where(srow == 0, 0.0, pltpu.roll(e_i, 1, 0))
        for s in range(SCAN_LEN):
            rows = slice(r_base + s * SUBLANES, r_base + (s + 1) * SUBLANES)
            t_r = atab_ref[s, :, re]
            t_i = atab_ref[s, :, im]
            hs_ref[rows, re] = hs_ref[rows, re] + (t_r * in_r - t_i * in_i)
            hs_ref[rows, im] = hs_ref[rows, im] + (t_r * in_i + t_i * in_r)
    assert not late_chunks
    for sb in range(N_SUB):
        for j in range(SSM_TILES):
            y_perm = _dot(hs_ref[sb * SUB:(sb + 1) * SUB, j * SSM_TILE_W:(j + 1) * SSM_TILE_W].astype(BF16),
                          cc_ref[j])
            for s in range(SCAN_LEN):
                yslab_ref[sb * SSM_TILES + j, pl.ds(s, SCAN_CHUNKS, stride=SCAN_LEN), :] = (
                    y_perm[s * SUBLANES:(s + 1) * SUBLANES, :])
    y_lin = jnp.concatenate(
        [jnp.concatenate([yslab_ref[sb * SSM_TILES + j] for j in range(SSM_TILES)], axis=1)
         for sb in range(N_SUB)], axis=0)
    ybr_ref[:, BRANCH_W:2 * BRANCH_W] = _s5_output(y_lin, proj_ref[:, UB0:UB0 + BRANCH_W], ptab_ref, gluw_ref)

    lane2 = lax.broadcasted_iota(jnp.int32, (2 * SUB, LANES), 1)
    kcol = lax.broadcasted_iota(jnp.int32, (SUB, 2 * SUB), 1)
    lane = lax.broadcasted_iota(jnp.int32, (SUB, LANES), 1)
    lo = lane < HEAD_DIM
    first_neg = jnp.where(ji == 0, NEG_INF, 0.0).astype(F32)
    kv_prev = kvprev_ref[...]
    gate_chunks = list(range(0, N_BRANCH * D_MODEL, PROJ_CT))
    per_tile = -(-len(gate_chunks) // (N_SUB * N_HEADS // 2))

    def project_gates(n):
        for _ in range(n):
            if gate_chunks:
                g0 = gate_chunks.pop(0)
                gates_ref[:, g0:g0 + PROJ_CT] = _gate_chunk(xb, w_in_ref, G0 + g0)

    for sb in range(N_SUB):
        rows = slice(sb * SUB, (sb + 1) * SUB)
        kv_cur = proj_ref[rows, K0:K0 + 2 * LANES]
        k_dup = _dup_halves(jnp.concatenate([kv_prev[:, 0:LANES], kv_cur[:, 0:LANES]], axis=0), lane2 < HEAD_DIM)
        v_dup = _dup_halves(jnp.concatenate([kv_prev[:, LANES:], kv_cur[:, LANES:]], axis=0), lane2 < HEAD_DIM)
        k_dup = [k.astype(BF16) for k in k_dup]
        v_dup = [v.astype(BF16) for v in v_dup]
        for t in range(N_HEADS // 2):
            project_gates(per_tile)
            qt = proj_ref[rows, Q0 + t * LANES:Q0 + (t + 1) * LANES]
            kv_head = (2 * t) // GQA_GROUP
            halves = []
            for half in range(2):
                h = 2 * t + half
                qm = jnp.where(lo if half == 0 else jnp.logical_not(lo), qt, 0.0).astype(BF16)
                s = _dot_nt(qm, k_dup[kv_head]) + bias_ref[h]
                if sb == 0:
                    s = s + jnp.where(kcol < SUB, first_neg, 0.0)
                sink = sinks_ref[layer, h]
                m = jnp.maximum(jnp.max(s, axis=-1, keepdims=True), sink)
                p = jnp.exp(s - m)
                denom = jnp.sum(p, axis=-1, keepdims=True) + jnp.exp(sink - m)
                o = _dot(p.astype(BF16), v_dup[kv_head])
                halves.append(o * (1.0 / denom))
            ybr_ref[rows, t * LANES:(t + 1) * LANES] = jnp.where(lo, halves[0], halves[1])
        kv_prev = kv_cur
    kvprev_ref[...] = kv_prev
    kvwin_ref[0] = kv_prev
    project_gates(len(gate_chunks))

    trow = lax.broadcasted_iota(jnp.int32, (CHUNK, CHUNK), 0)
    tcol = lax.broadcasted_iota(jnp.int32, (CHUNK, CHUNK), 1)
    tril = trow >= tcol
    for g in range(SGU_GROUPS):
        wm = jnp.where(tril, sguw_ref[g], 0.0).astype(BF16)
        b_row = ptab_ref[ROW_SGU_B:ROW_SGU_B + 1, g * CHUNK:(g + 1) * CHUNK]
        b_col = jnp.sum(jnp.where(trow == tcol, b_row, 0.0), axis=1, keepdims=True)
        for sb in range(N_SUB):
            rows = slice(sb * SUB, (sb + 1) * SUB)
            vn_g = proj_ref[rows, VC0 + g * LANES:VC0 + (g + 1) * LANES]
            sg = _dot(wm, vn_g.astype(BF16)) + b_col
            ybr_ref[rows, 2 * BRANCH_W + g * LANES:2 * BRANCH_W + (g + 1) * LANES] = (
                proj_ref[rows, UC0 + g * LANES:UC0 + (g + 1) * LANES] * sg)

    y_ref[0] = _merge(x_ref[0], w_in_ref, proj_ref, ybr_ref, merged_ref, wread_ref, wo_ref, ptab_ref, gates_ref)


def _layer_spec(layer, shape):
    nd = len(shape)
    return pl.BlockSpec((None,) + tuple(shape), lambda *_, _nd=nd: (layer,) + (0,) * _nd,
                        pipeline_mode=pl.Buffered(1))


def _prompt_layer(layer, x, sinks, lw):
    nb, t, _ = x.shape
    grid = (nb, t // TB)
    weights = (lw['w_in'], lw['a_row'], lw['bb'], lw['cc'], lw['ptab'], lw['glu_w'], lw['sgu_w'],
               lw['w_read'], lw['w_o'])
    in_specs = [pl.BlockSpec(memory_space=pltpu.SMEM),
                pl.BlockSpec((1, TB, D_MODEL), lambda b, j: (b, j, 0))]
    in_specs += [_layer_spec(layer, w.shape[1:]) for w in weights]
    out_shape = (jax.ShapeDtypeStruct((nb, t, D_MODEL), F32),
                 jax.ShapeDtypeStruct((nb, SUB, 2 * LANES), F32),
                 jax.ShapeDtypeStruct((nb, SUBLANES, 2 * SSM_W), F32))
    out_specs = (pl.BlockSpec((1, TB, D_MODEL), lambda b, j: (b, j, 0)),
                 pl.BlockSpec((1, SUB, 2 * LANES), lambda b, j: (b, 0, 0)),
                 pl.BlockSpec((1, SUBLANES, 2 * SSM_W), lambda b, j: (b, 0, 0)))
    scratch = [pltpu.VMEM((TB, G0), F32),
               pltpu.VMEM((TB, N_BRANCH * BRANCH_W), F32),
               pltpu.VMEM((SUB, 2 * LANES), F32),
               pltpu.VMEM((N_HEADS, SUB, 2 * SUB), F32),
               pltpu.VMEM((SCAN_LEN, SUBLANES, 2 * SSM_W), F32),
               pltpu.VMEM((2, SUBLANES, 2 * SSM_W), F32),
               pltpu.VMEM((TB, 2 * SSM_W), F32),
               pltpu.VMEM((SUBLANES, 2 * SSM_W), F32),
               pltpu.VMEM((N_SUB * SSM_TILES, SUB, LANES), F32),
               pltpu.VMEM((N_SUB * SSM_TILES, SUB, LANES), F32),
               pltpu.VMEM((TB, D_MODEL), BF16),
               pltpu.VMEM((TB, N_BRANCH * D_MODEL), F32)]
    return pl.pallas_call(
        functools.partial(_prompt_kernel, layer),
        grid=grid,
        in_specs=in_specs,
        out_specs=out_specs,
        out_shape=out_shape,
        scratch_shapes=scratch,
        compiler_params=pltpu.CompilerParams(
            dimension_semantics=("arbitrary", "arbitrary"),
            vmem_limit_bytes=VMEM_LIMIT_BYTES),
        name="prompt_layer",
    )(sinks, x, *weights)


def _sample_kernel(layer, sinks_ref, x_ref, ck_ref, cv_ref, h0_ref, w_in_ref, arow_ref, bb_ref, cc_ref,
                   ptab_ref, gluw_ref, sguw_ref, wread_ref, wo_ref,
                   y_ref, kv_ref, hout_ref, vn_ref,
                   proj_ref, ybr_ref, merged_ref):
    n_seq = x_ref.shape[0]
    win = ck_ref.shape[1]
    xb = x_ref[...].astype(BF16)
    for c0 in range(0, G0, PROJ_CT):
        _proj_chunk(xb, w_in_ref, proj_ref, c0)
    kv_ref[...] = proj_ref[:, K0:K0 + 2 * LANES]

    row = lax.broadcasted_iota(jnp.int32, (SUBLANES, LANES), 0)
    lane = lax.broadcasted_iota(jnp.int32, (SUBLANES, LANES), 1)
    lane_lo = lane < HEAD_DIM
    row_even = (row % 2) == 0
    sel = jnp.logical_or(jnp.logical_and(row_even, lane_lo),
                         jnp.logical_and(jnp.logical_not(row_even), jnp.logical_not(lane_lo)))
    swap = (row // GQA_GROUP) != (row % 2)
    slope = jnp.exp2(-(row.astype(F32) + 1.0))
    bias = -slope * (win - lane).astype(F32)
    sink = jnp.zeros((SUBLANES, 1), F32)
    for h in range(N_HEADS):
        sink = jnp.where(row[:, 0:1] == h, sinks_ref[layer, h], sink)

    def seq_group_step(gi, carry):
        r0 = pl.multiple_of(gi * SUBLANES, SUBLANES)
        q_rows = proj_ref[pl.ds(r0, SUBLANES), Q0:Q0 + BRANCH_W]
        k_rows = proj_ref[pl.ds(r0, SUBLANES), K0:K0 + LANES]
        v_rows = proj_ref[pl.ds(r0, SUBLANES), V0:V0 + LANES]
        q8s, scores = [], []
        for r in range(SUBLANES):
            q8 = jnp.zeros((SUBLANES, LANES), F32)
            for t in range(N_HEADS // 2):
                qt = jnp.broadcast_to(q_rows[r:r + 1, t * LANES:(t + 1) * LANES], (SUBLANES, LANES))
                q8 = jnp.where(jnp.logical_and(sel, row // 2 == t), qt, q8)
            q8 = jnp.where(swap, pltpu.roll(q8, HEAD_DIM, 1), q8)
            q8s.append(q8)
            scores.append(_dot_nt(q8.astype(BF16), ck_ref[r0 + r].astype(BF16)) + bias)
        s = jnp.concatenate(scores, axis=0)
        k_own = jnp.concatenate([jnp.broadcast_to(k_rows[r:r + 1, :], (SUBLANES, LANES))
                                 for r in range(SUBLANES)], axis=0)
        s_own = jnp.sum(jnp.concatenate(q8s, axis=0) * k_own, axis=-1, keepdims=True)
        sink_g = jnp.concatenate([sink] * SUBLANES, axis=0)
        m = jnp.maximum(jnp.maximum(jnp.max(s, axis=-1, keepdims=True), s_own), sink_g)
        p = jnp.exp(s - m)
        p_own = jnp.exp(s_own - m)
        inv = 1.0 / (jnp.sum(p, axis=-1, keepdims=True) + p_own + jnp.exp(sink_g - m))
        outs = [[] for _ in range(N_HEADS // 2)]
        for r in range(SUBLANES):
            rows8 = slice(r * SUBLANES, (r + 1) * SUBLANES)
            o = (_dot(p[rows8, :].astype(BF16), cv_ref[r0 + r].astype(BF16))
                 + p_own[rows8, :] * v_rows[r:r + 1, :]) * inv[rows8, :]
            o = jnp.where(swap, pltpu.roll(o, HEAD_DIM, 1), o)
            o = jnp.where(sel, o, 0.0)
            for t in range(N_HEADS // 2):
                outs[t].append(o[2 * t:2 * t + 1, :] + o[2 * t + 1:2 * t + 2, :])
        for t in range(N_HEADS // 2):
            ybr_ref[pl.ds(r0, SUBLANES), t * LANES:(t + 1) * LANES] = jnp.concatenate(outs[t], axis=0)
        return carry

    lax.fori_loop(0, n_seq // SUBLANES, seq_group_step, 0)

    vn = _sgu_layernorm(proj_ref[:, VC0:VC0 + BRANCH_W], ptab_ref)
    vn_ref[...] = vn
    for g in range(SGU_GROUPS):
        w00 = sguw_ref[g, 0:1, 0:1]
        b0 = ptab_ref[ROW_SGU_B:ROW_SGU_B + 1, g * CHUNK:g * CHUNK + 1]
        cols = slice(g * LANES, (g + 1) * LANES)
        ybr_ref[:, 2 * BRANCH_W + g * LANES:2 * BRANCH_W + (g + 1) * LANES] = (
            proj_ref[:, UC0 + g * LANES:UC0 + (g + 1) * LANES] * (w00 * vn[:, cols] + b0))

    u = proj_ref[:, UB0:UB0 + BRANCH_W]
    y_tiles = []
    for j in range(SSM_TILES):
        re, im = _re(j), _im(j)
        xs = _dot(u[:, j * LANES:(j + 1) * LANES].astype(BF16), bb_ref[j])
        a_r = arow_ref[:, re]
        a_i = arow_ref[:, im]
        h0r = h0_ref[:, re]
        h0i = h0_ref[:, im]
        hr = xs[:, 0:SSM_HALF] + (a_r * h0r - a_i * h0i)
        hi = xs[:, SSM_HALF:] + (a_r * h0i + a_i * h0r)
        hout_ref[:, re] = hr
        hout_ref[:, im] = hi
        y_tiles.append(_dot(jnp.concatenate([hr, hi], axis=1).astype(BF16), cc_ref[j]))
    y_lin = jnp.concatenate(y_tiles, axis=1)
    ybr_ref[:, BRANCH_W:2 * BRANCH_W] = _s5_output(y_lin, u, ptab_ref, gluw_ref)

    y_ref[...] = _merge(x_ref[...], w_in_ref, proj_ref, ybr_ref, merged_ref, wread_ref, wo_ref, ptab_ref)


def _sample_layer(layer, x, sinks, ck, cv, h0, lw):
    n = x.shape[0]
    per_layer = (ck, cv, h0, lw['w_in'], lw['a_row'], lw['bb'], lw['cc'], lw['ptab'], lw['glu_w'],
                 lw['sgu_w'], lw['w_read'], lw['w_o'])
    in_specs = [pl.BlockSpec(memory_space=pltpu.SMEM),
                pl.BlockSpec(x.shape, lambda i: (0, 0), pipeline_mode=pl.Buffered(1))]
    in_specs += [_layer_spec(layer, a.shape[1:]) for a in per_layer]
    out_shape = (jax.ShapeDtypeStruct((n, D_MODEL), F32),
                 jax.ShapeDtypeStruct((n, 2 * LANES), F32),
                 jax.ShapeDtypeStruct((n, 2 * SSM_W), F32),
                 jax.ShapeDtypeStruct((n, BRANCH_W), F32))
    out_specs = tuple(pl.BlockSpec(s.shape, lambda i: (0, 0)) for s in out_shape)
    scratch = [pltpu.VMEM((n, G0), F32), pltpu.VMEM((n, N_BRANCH * BRANCH_W), F32),
               pltpu.VMEM((n, D_MODEL), BF16)]
    return pl.pallas_call(
        functools.partial(_sample_kernel, layer),
        grid=(1,),
        in_specs=in_specs,
        out_specs=out_specs,
        out_shape=out_shape,
        scratch_shapes=scratch,
        compiler_params=pltpu.CompilerParams(
            dimension_semantics=("arbitrary",), vmem_limit_bytes=VMEM_LIMIT_BYTES),
        name="sample_layer",
    )(sinks, x, *per_layer)


def _param_table(ssm_d, glu_b, sgu_ln_g, sgu_ln_b, ln_g, ln_b, sgu_b):
    depth = ln_g.shape[0]
    rows = [jnp.concatenate([ssm_d, glu_b], axis=1),
            jnp.concatenate([sgu_ln_g, sgu_ln_b], axis=1),
            ln_g, ln_b,
            jnp.concatenate([sgu_b.reshape(depth, BRANCH_W), jnp.zeros((depth, BRANCH_W), F32)], axis=1)]
    rows += [jnp.zeros((depth, D_MODEL), F32)] * (PARAM_ROWS - len(rows))
    return jnp.stack(rows, axis=1)


def kernel(x_prompt, x_sample, cache_k_win, cache_v_win, state_ssm_re, state_ssm_im, w_in, attn_sinks,
           ssm_lambda_re, ssm_lambda_im, ssm_log_dt, ssm_b_re, ssm_b_im, ssm_c_re, ssm_c_im, ssm_d,
           glu_w, glu_b, sgu_ln_g, sgu_ln_b, sgu_w, sgu_b, w_read, w_o, ln_g, ln_b):
    nb, t, _ = x_prompt.shape
    ns = x_sample.shape[0]
    depth, _, win = cache_k_win.shape[:3]
    assert t % TB == 0 and win == WINDOW == SUB and ns % SUBLANES == 0

    a_row, bb, cc = _ssm_prep(ssm_lambda_re, ssm_lambda_im, ssm_log_dt, ssm_b_re, ssm_b_im, ssm_c_re, ssm_c_im)
    lw = dict(w_in=w_in.astype(BF16), a_row=a_row, bb=bb, cc=cc,
              ptab=_param_table(ssm_d, glu_b, sgu_ln_g, sgu_ln_b, ln_g, ln_b, sgu_b),
              glu_w=glu_w.astype(BF16), sgu_w=sgu_w, w_read=w_read.astype(BF16), w_o=w_o.astype(BF16))
    ck = cache_k_win.reshape(depth, ns, win, KV_HEADS * HEAD_DIM)
    cv = cache_v_win.reshape(depth, ns, win, KV_HEADS * HEAD_DIM)
    h0 = _state_to_kernel_layout(state_ssm_re, state_ssm_im)

    xp = x_prompt
    xs = x_sample.reshape(ns, D_MODEL)
    kp, vp, hrp, hip = [], [], [], []
    ksm, vsm, hrs, his, vcs = [], [], [], [], []
    for l in range(depth):
        xp, kvwin, hstate = _prompt_layer(l, xp, attn_sinks, lw)
        kp.append(kvwin[:, :, 0:LANES].reshape(nb, SUB, KV_HEADS, HEAD_DIM))
        vp.append(kvwin[:, :, LANES:].reshape(nb, SUB, KV_HEADS, HEAD_DIM))
        h_re, h_im = _state_from_kernel_layout(hstate[:, SUBLANES - 1, :])
        hrp.append(h_re)
        hip.append(h_im)

        xs, kv_s, h_s, vn_s = _sample_layer(l, xs, attn_sinks, ck, cv, h0, lw)
        ksm.append(kv_s[:, 0:LANES].reshape(ns, 1, KV_HEADS, HEAD_DIM))
        vsm.append(kv_s[:, LANES:].reshape(ns, 1, KV_HEADS, HEAD_DIM))
        h_re, h_im = _state_from_kernel_layout(h_s)
        hrs.append(h_re)
        his.append(h_im)
        vcs.append(vn_s.reshape(ns, 1, BRANCH_W))
    return (xp, xs.reshape(ns, 1, D_MODEL), jnp.stack(kp), jnp.stack(vp), jnp.stack(hrp), jnp.stack(hip),
            jnp.stack(ksm), jnp.stack(vsm), jnp.stack(hrs), jnp.stack(his), jnp.stack(vcs))
```

```python
import functools
import math

import jax
import jax.numpy as jnp
from jax import lax
from jax.experimental import pallas as pl
from jax.experimental.pallas import tpu as pltpu

D_MODEL = 1024
BRANCH_W = 512
HEAD_DIM = 64
N_HEADS = 8
KV_HEADS = 2
GQA_GROUP = N_HEADS // KV_HEADS
WINDOW = 128
SSM_GROUP_CH = 16
SSM_GROUPS = 32
SSM_STATE = 64
SSM_W = SSM_GROUPS * SSM_STATE
CHUNK = 128
SGU_GROUPS = 4
N_BRANCH = 3
DEPTH = 2
ALPHA = (2 * DEPTH) ** 0.25
LN_EPS = 1e-5
ATTN_SCALE = HEAD_DIM ** -0.5
NEG_INF = -1e30

Q0, K0, V0, ZA0, UB0, ZB0, UC0, VC0, ZC0, G0 = 0, 512, 640, 768, 1280, 1792, 2304, 2816, 3328, 3840
N_IN = 6912

LANES = 128
SUBLANES = 8
MXU_COLS = 256
VMEM_LIMIT_BYTES = 60 * 1024 * 1024

SUB = 128
N_SUB = 2
TB = N_SUB * SUB
PROJ_CT = MXU_COLS

SSM_TILES = BRANCH_W // LANES
SSM_TILE_GROUPS = SSM_GROUPS // SSM_TILES
SSM_HALF = SSM_W // SSM_TILES
SSM_TILE_W = 2 * SSM_HALF
SCAN_CHUNKS = SUBLANES
SCAN_LEN = SUB // SCAN_CHUNKS

ROW_DSKIP_GLUB, ROW_SGU_LN, ROW_LN_G, ROW_LN_B, ROW_SGU_B = 0, 1, 2, 3, 4
PARAM_ROWS = SUBLANES

F32 = jnp.float32
BF16 = jnp.bfloat16


def _re(j):
    return slice(j * SSM_TILE_W, j * SSM_TILE_W + SSM_HALF)


def _im(j):
    return slice(j * SSM_TILE_W + SSM_HALF, (j + 1) * SSM_TILE_W)


def _sigmoid(x):
    return 1.0 / (1.0 + jnp.exp(-x))


def _silu(x):
    return x * _sigmoid(x)


def _gelu_tanh(x):
    c = math.sqrt(2.0 / math.pi)
    return 0.5 * x * (1.0 + jnp.tanh(c * (x + 0.044715 * (x * x * x))))


def _layernorm(x, g, b):
    mu = jnp.mean(x, axis=-1, keepdims=True)
    xc = x - mu
    var = jnp.mean(xc * xc, axis=-1, keepdims=True)
    return xc * lax.rsqrt(var + LN_EPS) * g + b


def _dot(a, b):
    return jnp.dot(a, b, preferred_element_type=F32)


def _dot_nt(a, b):
    return lax.dot_general(a, b, (((1,), (1,)), ((), ())), preferred_element_type=F32)


def _slope(h):
    return 2.0 ** (-(h + 1.0))


def _dup_halves(x, lo):
    sw = pltpu.roll(x, HEAD_DIM, 1)
    return jnp.where(lo, x, sw), jnp.where(lo, sw, x)


def _ssm_prep_kernel(lr_ref, li_ref, ldt_ref, brt_ref, bit_ref, crt_ref, cit_ref,
                     arow_ref, bb_ref, cc_ref, bbs_ref, ccs_ref):
    lr = lr_ref[...]
    li = li_ref[...]
    dt = jnp.exp(ldt_ref[...])
    mag = jnp.exp(lr * dt)
    ar = mag * jnp.cos(li * dt)
    ai = mag * jnp.sin(li * dt)
    den = lr * lr + li * li
    cr = ((ar - 1.0) * lr + ai * li) / den
    ci = (ai * lr - (ar - 1.0) * li) / den
    for j in range(SSM_TILES):
        bbs_ref[...] = jnp.zeros((LANES, SSM_TILE_W), F32)
        ccs_ref[...] = jnp.zeros((SSM_TILE_W, LANES), F32)
        for gl in range(SSM_TILE_GROUPS):
            g = j * SSM_TILE_GROUPS + gl
            cols = slice(gl * SSM_STATE, (gl + 1) * SSM_STATE)
            cols_im = slice(SSM_HALF + gl * SSM_STATE, SSM_HALF + (gl + 1) * SSM_STATE)
            ch = slice(gl * SSM_GROUP_CH, (gl + 1) * SSM_GROUP_CH)
            arow_ref[:, j * SSM_TILE_W + gl * SSM_STATE:j * SSM_TILE_W + (gl + 1) * SSM_STATE] = ar[g:g + 1, :]
            arow_ref[:, j * SSM_TILE_W + SSM_HALF + gl * SSM_STATE:
                     j * SSM_TILE_W + SSM_HALF + (gl + 1) * SSM_STATE] = ai[g:g + 1, :]
            br = brt_ref[g]
            bi = bit_ref[g]
            crg = cr[g:g + 1, :]
            cig = ci[g:g + 1, :]
            bbs_ref[ch, cols] = crg * br - cig * bi
            bbs_ref[ch, cols_im] = crg * bi + cig * br
            ccs_ref[cols, ch] = crt_ref[g]
            ccs_ref[cols_im, ch] = -cit_ref[g]
        bb_ref[j] = bbs_ref[...].astype(BF16)
        cc_ref[j] = ccs_ref[...].astype(BF16)


def _ssm_prep(lam_re, lam_im, log_dt, b_re, b_im, c_re, c_im):
    depth = lam_re.shape[0]
    brt = jnp.swapaxes(b_re, 2, 3)
    bit = jnp.swapaxes(b_im, 2, 3)
    crt = jnp.swapaxes(c_re, 2, 3)
    cit = jnp.swapaxes(c_im, 2, 3)

    def spec(*shape):
        nd = len(shape)
        return pl.BlockSpec((None,) + shape, lambda l, _nd=nd: (l,) + (0,) * _nd)

    return pl.pallas_call(
        _ssm_prep_kernel,
        grid=(depth,),
        in_specs=[spec(SSM_GROUPS, SSM_STATE), spec(SSM_GROUPS, SSM_STATE), spec(SSM_GROUPS, 1),
                  spec(SSM_GROUPS, SSM_GROUP_CH, SSM_STATE), spec(SSM_GROUPS, SSM_GROUP_CH, SSM_STATE),
                  spec(SSM_GROUPS, SSM_STATE, SSM_GROUP_CH), spec(SSM_GROUPS, SSM_STATE, SSM_GROUP_CH)],
        out_specs=(spec(1, 2 * SSM_W), spec(SSM_TILES, LANES, SSM_TILE_W), spec(SSM_TILES, SSM_TILE_W, LANES)),
        out_shape=(jax.ShapeDtypeStruct((depth, 1, 2 * SSM_W), F32),
                   jax.ShapeDtypeStruct((depth, SSM_TILES, LANES, SSM_TILE_W), BF16),
                   jax.ShapeDtypeStruct((depth, SSM_TILES, SSM_TILE_W, LANES), BF16)),
        scratch_shapes=[pltpu.VMEM((LANES, SSM_TILE_W), F32), pltpu.VMEM((SSM_TILE_W, LANES), F32)],
        compiler_params=pltpu.CompilerParams(dimension_semantics=("arbitrary",)),
        name="ssm_prep",
    )(lam_re, lam_im, log_dt.reshape(depth, SSM_GROUPS, 1), brt, bit, crt, cit)


def _state_to_kernel_layout(h_re, h_im):
    lead = h_re.shape[:-2]
    return jnp.stack([h_re.reshape(lead + (SSM_TILES, SSM_HALF)), h_im.reshape(lead + (SSM_TILES, SSM_HALF))],
                     axis=-2).reshape(lead + (2 * SSM_W,))


def _state_from_kernel_layout(h):
    n = h.shape[0]
    h4 = h.reshape(n, SSM_TILES, 2, SSM_HALF)
    return (h4[:, :, 0].reshape(n, SSM_GROUPS, SSM_STATE), h4[:, :, 1].reshape(n, SSM_GROUPS, SSM_STATE))


def _proj_chunk(xb, w_in_ref, proj_ref, c0):
    assert c0 + PROJ_CT <= G0
    y = _dot(xb, w_in_ref[:, c0:c0 + PROJ_CT])
    if any(z0 <= c0 < z0 + BRANCH_W for z0 in (ZA0, ZB0, ZC0)):
        y = _silu(y)
    elif c0 < K0:
        y = y * ATTN_SCALE
    proj_ref[:, c0:c0 + PROJ_CT] = y


def _s5_output(y_lin, u, ptab_ref, gluw_ref):
    y = y_lin + ptab_ref[ROW_DSKIP_GLUB:ROW_DSKIP_GLUB + 1, 0:BRANCH_W] * u
    y = _gelu_tanh(y)
    glu_b = ptab_ref[ROW_DSKIP_GLUB:ROW_DSKIP_GLUB + 1, BRANCH_W:2 * BRANCH_W]
    return y * _sigmoid(_dot(y.astype(BF16), gluw_ref[...]) + glu_b)


def _gate_chunk(xb, w_in_ref, g0):
    return _sigmoid(_dot(xb, w_in_ref[:, g0:g0 + PROJ_CT]))


def _merge(x, w_in_ref, proj_ref, ybr_ref, merged_ref, wread_ref, wo_ref, ptab_ref, gates_ref=None):
    xb = x.astype(BF16)
    ybs = [(ybr_ref[:, b * BRANCH_W:(b + 1) * BRANCH_W] * proj_ref[:, z0:z0 + BRANCH_W]).astype(BF16)
           for b, z0 in enumerate((ZA0, ZB0, ZC0))]
    for n0 in range(0, D_MODEL, PROJ_CT):
        merged = None
        for b in range(N_BRANCH):
            g0 = b * D_MODEL + n0
            gate = (_gate_chunk(xb, w_in_ref, G0 + g0) if gates_ref is None
                    else gates_ref[:, g0:g0 + PROJ_CT])
            term = gate * _dot(ybs[b], wread_ref[b, :, n0:n0 + PROJ_CT])
            merged = term if merged is None else merged + term
        merged_ref[:, n0:n0 + PROJ_CT] = merged.astype(BF16)
    out = _dot(merged_ref[...], wo_ref[...])
    return _layernorm(ALPHA * x + out, ptab_ref[ROW_LN_G:ROW_LN_G + 1, :], ptab_ref[ROW_LN_B:ROW_LN_B + 1, :])


def _sgu_layernorm(v, ptab_ref):
    return _layernorm(v, ptab_ref[ROW_SGU_LN:ROW_SGU_LN + 1, 0:BRANCH_W],
                      ptab_ref[ROW_SGU_LN:ROW_SGU_LN + 1, BRANCH_W:2 * BRANCH_W])


def _prompt_kernel(layer, sinks_ref, x_ref, w_in_ref, arow_ref, bb_ref, cc_ref, ptab_ref, gluw_ref,
                   sguw_ref, wread_ref, wo_ref,
                   y_ref, kvwin_ref, hstate_ref,
                   proj_ref, ybr_ref, kvprev_ref, bias_ref, atab_ref, apw_ref, hs_ref, hcarry_ref,
                   uslab_ref, yslab_ref, merged_ref, gates_ref, hsb_ref):
    bi = pl.program_id(0)
    ji = pl.program_id(1)

    @pl.when(jnp.logical_and(bi == 0, ji == 0))
    def _init_tables():
        qa = lax.broadcasted_iota(jnp.int32, (SUB, 2 * SUB), 0)
        kk = lax.broadcasted_iota(jnp.int32, (SUB, 2 * SUB), 1)
        dist = qa + WINDOW - kk
        valid = jnp.logical_and(dist >= 0, dist <= WINDOW)
        distf = dist.astype(F32)
        for h in range(N_HEADS):
            bias_ref[h] = jnp.where(valid, -_slope(h) * distf, NEG_INF)
        for j in range(SSM_TILES):
            a_r = arow_ref[:, _re(j)]
            a_i = arow_ref[:, _im(j)]
            pr, pi = a_r, a_i
            for s in range(SCAN_LEN):
                atab_ref[s, :, _re(j)] = jnp.broadcast_to(pr, (SUBLANES, SSM_HALF))
                atab_ref[s, :, _im(j)] = jnp.broadcast_to(pi, (SUBLANES, SSM_HALF))
                if s + 1 < SCAN_LEN:
                    pr, pi = pr * a_r - pi * a_i, pr * a_i + pi * a_r
            for k in range(2):
                pr, pi = pr * pr - pi * pi, 2.0 * (pr * pi)
                apw_ref[k, :, _re(j)] = jnp.broadcast_to(pr, (SUBLANES, SSM_HALF))
                apw_ref[k, :, _im(j)] = jnp.broadcast_to(pi, (SUBLANES, SSM_HALF))

    @pl.when(ji == 0)
    def _reset_carries():
        kvprev_ref[...] = jnp.zeros((SUB, 2 * LANES), F32)
        hcarry_ref[...] = jnp.zeros((SUBLANES, 2 * SSM_W), F32)

    xb = x_ref[0].astype(BF16)
    late_chunks = [z0 + c for z0 in (ZA0, ZB0, ZC0) for c in range(0, BRANCH_W, PROJ_CT)]

    def project_late(n):
        for _ in range(n):
            if late_chunks:
                _proj_chunk(xb, w_in_ref, proj_ref, late_chunks.pop(0))

    for c0 in range(UB0, UB0 + BRANCH_W, PROJ_CT):
        _proj_chunk(xb, w_in_ref, proj_ref, c0)

    for sb in range(N_SUB):
        for j in range(SSM_TILES):
            uslab_ref[sb * SSM_TILES + j] = proj_ref[sb * SUB:(sb + 1) * SUB, UB0 + j * LANES:UB0 + (j + 1) * LANES]
        for j in range(SSM_TILES):
            u_perm = jnp.concatenate(
                [uslab_ref[sb * SSM_TILES + j, pl.ds(s, SCAN_CHUNKS, stride=SCAN_LEN), :]
                 for s in range(SCAN_LEN)], axis=0)
            hs_ref[sb * SUB:(sb + 1) * SUB, j * SSM_TILE_W:(j + 1) * SSM_TILE_W] = _dot(
                u_perm.astype(BF16), bb_ref[j])

    for c0 in list(range(Q0, ZA0, PROJ_CT)) + list(range(UC0, ZC0, PROJ_CT)):
        _proj_chunk(xb, w_in_ref, proj_ref, c0)
    proj_ref[:, VC0:VC0 + BRANCH_W] = _sgu_layernorm(proj_ref[:, VC0:VC0 + BRANCH_W], ptab_ref)

    srow = lax.broadcasted_iota(jnp.int32, (SUBLANES, SSM_HALF), 0)
    for sb, j in [(sb, j) for sb in range(N_SUB) for j in range(SSM_TILES)]:
        r_base = sb * SUB
        re, im = _re(j), _im(j)
        a_r = atab_ref[0, :, re]
        a_i = atab_ref[0, :, im]
        project_late(1)
        c_r = jnp.broadcast_to(hcarry_ref[SUBLANES - 1:SUBLANES, re], (SUBLANES, SSM_HALF))
        c_i = jnp.broadcast_to(hcarry_ref[SUBLANES - 1:SUBLANES, im], (SUBLANES, SSM_HALF))
        h_r = hs_ref[r_base:r_base + SUBLANES, re] + jnp.where(srow == 0, a_r * c_r - a_i * c_i, 0.0)
        h_i = hs_ref[r_base:r_base + SUBLANES, im] + jnp.where(srow == 0, a_r * c_i + a_i * c_r, 0.0)
        for s in range(1, SCAN_LEN):
            rows = slice(r_base + s * SUBLANES, r_base + (s + 1) * SUBLANES)
            h_r, h_i = (hs_ref[rows, re] + (a_r * h_r - a_i * h_i),
                        hs_ref[rows, im] + (a_r * h_i + a_i * h_r))
        e_r, e_i = h_r, h_i
        for k, d in enumerate((1, 2, 4)):
            if k == 0:
                p_r, p_i = atab_ref[SCAN_LEN - 1, :, re], atab_ref[SCAN_LEN - 1, :, im]
            else:
                p_r, p_i = apw_ref[k - 1, :, re], apw_ref[k - 1, :, im]
            s_r = pltpu.roll(e_r, d, 0)
            s_i = pltpu.roll(e_i, d, 0)
            e_r, e_i = (e_r + jnp.where(srow >= d, p_r * s_r - p_i * s_i, 0.0),
                        e_i + jnp.where(srow >= d, p_r * s_i + p_i * s_r, 0.0))
        hcarry_ref[:, re] = e_r
        hcarry_ref[:, im] = e_i
        hstate_ref[0, :, re] = e_r
        hstate_ref[0, :, im] = e_i
        h_r = jnp.where(srow == 0, c_r, pltpu.roll(e_r, 1, 0))
        h_i = jnp.where(srow == 0, c_i, pltpu.roll(e_i, 1, 0))
        for s in range(0, SCAN_LEN, 2):
            pair_r, pair_i = [], []
            for s1 in (s, s + 1):
                rows = slice(r_base + s1 * SUBLANES, r_base + (s1 + 1) * SUBLANES)
                h_r, h_i = (hs_ref[rows, re] + (a_r * h_r - a_i * h_i),
                            hs_ref[rows, im] + (a_r * h_i + a_i * h_r))
                pair_r.append(h_r)
                pair_i.append(h_i)
            rows = slice(r_base + s * SUBLANES, r_base + (s + 2) * SUBLANES)
            hsb_ref[rows, re] = jnp.concatenate(pair_r, axis=0).astype(BF16)
            hsb_ref[rows, im] = jnp.concatenate(pair_i, axis=0).astype(BF16)
    assert not late_chunks
    for sb in range(N_SUB):
        for j in range(SSM_TILES):
            y_perm = _dot(hsb_ref[sb * SUB:(sb + 1) * SUB, j * SSM_TILE_W:(j + 1) * SSM_TILE_W],
                          cc_ref[j])
            for s in range(SCAN_LEN):
                yslab_ref[sb * SSM_TILES + j, pl.ds(s, SCAN_CHUNKS, stride=SCAN_LEN), :] = (
                    y_perm[s * SUBLANES:(s + 1) * SUBLANES, :])
    y_lin = jnp.concatenate(
        [jnp.concatenate([yslab_ref[sb * SSM_TILES + j] for j in range(SSM_TILES)], axis=1)
         for sb in range(N_SUB)], axis=0)
    ybr_ref[:, BRANCH_W:2 * BRANCH_W] = _s5_output(y_lin, proj_ref[:, UB0:UB0 + BRANCH_W], ptab_ref, gluw_ref)

    lane2 = lax.broadcasted_iota(jnp.int32, (2 * SUB, LANES), 1)
    kcol = lax.broadcasted_iota(jnp.int32, (SUB, 2 * SUB), 1)
    lane = lax.broadcasted_iota(jnp.int32, (SUB, LANES), 1)
    lo = lane < HEAD_DIM
    first_neg = jnp.where(ji == 0, NEG_INF, 0.0).astype(F32)
    kv_prev = kvprev_ref[...]
    gate_chunks = list(range(0, N_BRANCH * D_MODEL, PROJ_CT))
    per_tile = -(-len(gate_chunks) // (N_SUB * N_HEADS // 2))

    def project_gates(n):
        for _ in range(n):
            if gate_chunks:
                g0 = gate_chunks.pop(0)
                gates_ref[:, g0:g0 + PROJ_CT] = _gate_chunk(xb, w_in_ref, G0 + g0)

    for sb in range(N_SUB):
        rows = slice(sb * SUB, (sb + 1) * SUB)
        kv_cur = proj_ref[rows, K0:K0 + 2 * LANES]
        k_dup = _dup_halves(jnp.concatenate([kv_prev[:, 0:LANES], kv_cur[:, 0:LANES]], axis=0), lane2 < HEAD_DIM)
        v_dup = _dup_halves(jnp.concatenate([kv_prev[:, LANES:], kv_cur[:, LANES:]], axis=0), lane2 < HEAD_DIM)
        k_dup = [k.astype(BF16) for k in k_dup]
        v_dup = [v.astype(BF16) for v in v_dup]
        for t in range(N_HEADS // 2):
            project_gates(per_tile)
            qt = proj_ref[rows, Q0 + t * LANES:Q0 + (t + 1) * LANES]
            kv_head = (2 * t) // GQA_GROUP
            halves = []
            for half in range(2):
                h = 2 * t + half
                qm = jnp.where(lo if half == 0 else jnp.logical_not(lo), qt, 0.0).astype(BF16)
                s = _dot_nt(qm, k_dup[kv_head]) + bias_ref[h]
                if sb == 0:
                    s = s + jnp.where(kcol < SUB, first_neg, 0.0)
                sink = sinks_ref[layer, h]
                m = jnp.maximum(jnp.max(s, axis=-1, keepdims=True), sink)
                p = jnp.exp(s - m)
                denom = jnp.sum(p, axis=-1, keepdims=True) + jnp.exp(sink - m)
                o = _dot(p.astype(BF16), v_dup[kv_head])
                halves.append(o * (1.0 / denom))
            ybr_ref[rows, t * LANES:(t + 1) * LANES] = jnp.where(lo, halves[0], halves[1])
        kv_prev = kv_cur
    kvprev_ref[...] = kv_prev
    kvwin_ref[0] = kv_prev
    project_gates(len(gate_chunks))

    trow = lax.broadcasted_iota(jnp.int32, (CHUNK, CHUNK), 0)
    tcol = lax.broadcasted_iota(jnp.int32, (CHUNK, CHUNK), 1)
    tril = trow >= tcol
    for g in range(SGU_GROUPS):
        wm = jnp.where(tril, sguw_ref[g], 0.0).astype(BF16)
        b_row = ptab_ref[ROW_SGU_B:ROW_SGU_B + 1, g * CHUNK:(g + 1) * CHUNK]
        b_col = jnp.sum(jnp.where(trow == tcol, b_row, 0.0), axis=1, keepdims=True)
        for sb in range(N_SUB):
            rows = slice(sb * SUB, (sb + 1) * SUB)
            vn_g = proj_ref[rows, VC0 + g * LANES:VC0 + (g + 1) * LANES]
            sg = _dot(wm, vn_g.astype(BF16)) + b_col
            ybr_ref[rows, 2 * BRANCH_W + g * LANES:2 * BRANCH_W + (g + 1) * LANES] = (
                proj_ref[rows, UC0 + g * LANES:UC0 + (g + 1) * LANES] * sg)

    y_ref[0] = _merge(x_ref[0], w_in_ref, proj_ref, ybr_ref, merged_ref, wread_ref, wo_ref, ptab_ref, gates_ref)


def _layer_spec(layer, shape):
    nd = len(shape)
    return pl.BlockSpec((None,) + tuple(shape), lambda *_, _nd=nd: (layer,) + (0,) * _nd,
                        pipeline_mode=pl.Buffered(1))


def _prompt_layer(layer, x, sinks, lw):
    nb, t, _ = x.shape
    grid = (nb, t // TB)
    weights = (lw['w_in'], lw['a_row'], lw['bb'], lw['cc'], lw['ptab'], lw['glu_w'], lw['sgu_w'],
               lw['w_read'], lw['w_o'])
    in_specs = [pl.BlockSpec(memory_space=pltpu.SMEM),
                pl.BlockSpec((1, TB, D_MODEL), lambda b, j: (b, j, 0))]
    in_specs += [_layer_spec(layer, w.shape[1:]) for w in weights]
    out_shape = (jax.ShapeDtypeStruct((nb, t, D_MODEL), F32),
                 jax.ShapeDtypeStruct((nb, SUB, 2 * LANES), F32),
                 jax.ShapeDtypeStruct((nb, SUBLANES, 2 * SSM_W), F32))
    out_specs = (pl.BlockSpec((1, TB, D_MODEL), lambda b, j: (b, j, 0)),
                 pl.BlockSpec((1, SUB, 2 * LANES), lambda b, j: (b, 0, 0)),
                 pl.BlockSpec((1, SUBLANES, 2 * SSM_W), lambda b, j: (b, 0, 0)))
    scratch = [pltpu.VMEM((TB, G0), F32),
               pltpu.VMEM((TB, N_BRANCH * BRANCH_W), F32),
               pltpu.VMEM((SUB, 2 * LANES), F32),
               pltpu.VMEM((N_HEADS, SUB, 2 * SUB), F32),
               pltpu.VMEM((SCAN_LEN, SUBLANES, 2 * SSM_W), F32),
               pltpu.VMEM((2, SUBLANES, 2 * SSM_W), F32),
               pltpu.VMEM((TB, 2 * SSM_W), F32),
               pltpu.VMEM((SUBLANES, 2 * SSM_W), F32),
               pltpu.VMEM((N_SUB * SSM_TILES, SUB, LANES), F32),
               pltpu.VMEM((N_SUB * SSM_TILES, SUB, LANES), F32),
               pltpu.VMEM((TB, D_MODEL), BF16),
               pltpu.VMEM((TB, N_BRANCH * D_MODEL), F32),
               pltpu.VMEM((TB, 2 * SSM_W), BF16)]
    return pl.pallas_call(
        functools.partial(_prompt_kernel, layer),
        grid=grid,
        in_specs=in_specs,
        out_specs=out_specs,
        out_shape=out_shape,
        scratch_shapes=scratch,
        compiler_params=pltpu.CompilerParams(
            dimension_semantics=("arbitrary", "arbitrary"),
            vmem_limit_bytes=VMEM_LIMIT_BYTES),
        name="prompt_layer",
    )(sinks, x, *weights)


def _sample_kernel(layer, sinks_ref, x_ref, ck_ref, cv_ref, h0_ref, w_in_ref, arow_ref, bb_ref, cc_ref,
                   ptab_ref, gluw_ref, sguw_ref, wread_ref, wo_ref,
                   y_ref, kv_ref, hout_ref, vn_ref,
                   proj_ref, ybr_ref, merged_ref):
    n_seq = x_ref.shape[0]
    win = ck_ref.shape[1]
    xb = x_ref[...].astype(BF16)
    for c0 in range(0, G0, PROJ_CT):
        _proj_chunk(xb, w_in_ref, proj_ref, c0)
    kv_ref[...] = proj_ref[:, K0:K0 + 2 * LANES]

    row = lax.broadcasted_iota(jnp.int32, (SUBLANES, LANES), 0)
    lane = lax.broadcasted_iota(jnp.int32, (SUBLANES, LANES), 1)
    lane_lo = lane < HEAD_DIM
    row_even = (row % 2) == 0
    sel = jnp.logical_or(jnp.logical_and(row_even, lane_lo),
                         jnp.logical_and(jnp.logical_not(row_even), jnp.logical_not(lane_lo)))
    swap = (row // GQA_GROUP) != (row % 2)
    slope = jnp.exp2(-(row.astype(F32) + 1.0))
    bias = -slope * (win - lane).astype(F32)
    sink = jnp.zeros((SUBLANES, 1), F32)
    for h in range(N_HEADS):
        sink = jnp.where(row[:, 0:1] == h, sinks_ref[layer, h], sink)

    def seq_group_step(gi, carry):
        r0 = pl.multiple_of(gi * SUBLANES, SUBLANES)
        q_rows = proj_ref[pl.ds(r0, SUBLANES), Q0:Q0 + BRANCH_W]
        k_rows = proj_ref[pl.ds(r0, SUBLANES), K0:K0 + LANES]
        v_rows = proj_ref[pl.ds(r0, SUBLANES), V0:V0 + LANES]
        q8s, scores = [], []
        for r in range(SUBLANES):
            q8 = jnp.zeros((SUBLANES, LANES), F32)
            for t in range(N_HEADS // 2):
                qt = jnp.broadcast_to(q_rows[r:r + 1, t * LANES:(t + 1) * LANES], (SUBLANES, LANES))
                q8 = jnp.where(jnp.logical_and(sel, row // 2 == t), qt, q8)
            q8 = jnp.where(swap, pltpu.roll(q8, HEAD_DIM, 1), q8)
            q8s.append(q8)
            scores.append(_dot_nt(q8.astype(BF16), ck_ref[r0 + r].astype(BF16)) + bias)
        s = jnp.concatenate(scores, axis=0)
        k_own = jnp.concatenate([jnp.broadcast_to(k_rows[r:r + 1, :], (SUBLANES, LANES))
                                 for r in range(SUBLANES)], axis=0)
        s_own = jnp.sum(jnp.concatenate(q8s, axis=0) * k_own, axis=-1, keepdims=True)
        sink_g = jnp.concatenate([sink] * SUBLANES, axis=0)
        m = jnp.maximum(jnp.maximum(jnp.max(s, axis=-1, keepdims=True), s_own), sink_g)
        p = jnp.exp(s - m)
        p_own = jnp.exp(s_own - m)
        inv = 1.0 / (jnp.sum(p, axis=-1, keepdims=True) + p_own + jnp.exp(sink_g - m))
        outs = [[] for _ in range(N_HEADS // 2)]
        for r in range(SUBLANES):
            rows8 = slice(r * SUBLANES, (r + 1) * SUBLANES)
            o = (_dot(p[rows8, :].astype(BF16), cv_ref[r0 + r].astype(BF16))
                 + p_own[rows8, :] * v_rows[r:r + 1, :]) * inv[rows8, :]
            o = jnp.where(swap, pltpu.roll(o, HEAD_DIM, 1), o)
            o = jnp.where(sel, o, 0.0)
            for t in range(N_HEADS // 2):
                outs[t].append(o[2 * t:2 * t + 1, :] + o[2 * t + 1:2 * t + 2, :])
        for t in range(N_HEADS // 2):
            ybr_ref[pl.ds(r0, SUBLANES), t * LANES:(t + 1) * LANES] = jnp.concatenate(outs[t], axis=0)
        return carry

    lax.fori_loop(0, n_seq // SUBLANES, seq_group_step, 0, unroll=2)

    vn = _sgu_layernorm(proj_ref[:, VC0:VC0 + BRANCH_W], ptab_ref)
    vn_ref[...] = vn
    for g in range(SGU_GROUPS):
        w00 = sguw_ref[g, 0:1, 0:1]
        b0 = ptab_ref[ROW_SGU_B:ROW_SGU_B + 1, g * CHUNK:g * CHUNK + 1]
        cols = slice(g * LANES, (g + 1) * LANES)
        ybr_ref[:, 2 * BRANCH_W + g * LANES:2 * BRANCH_W + (g + 1) * LANES] = (
            proj_ref[:, UC0 + g * LANES:UC0 + (g + 1) * LANES] * (w00 * vn[:, cols] + b0))

    u = proj_ref[:, UB0:UB0 + BRANCH_W]
    y_tiles = []
    for j in range(SSM_TILES):
        re, im = _re(j), _im(j)
        xs = _dot(u[:, j * LANES:(j + 1) * LANES].astype(BF16), bb_ref[j])
        a_r = arow_ref[:, re]
        a_i = arow_ref[:, im]
        h0r = h0_ref[:, re]
        h0i = h0_ref[:, im]
        hr = xs[:, 0:SSM_HALF] + (a_r * h0r - a_i * h0i)
        hi = xs[:, SSM_HALF:] + (a_r * h0i + a_i * h0r)
        hout_ref[:, re] = hr
        hout_ref[:, im] = hi
        y_tiles.append(_dot(jnp.concatenate([hr, hi], axis=1).astype(BF16), cc_ref[j]))
    y_lin = jnp.concatenate(y_tiles, axis=1)
    ybr_ref[:, BRANCH_W:2 * BRANCH_W] = _s5_output(y_lin, u, ptab_ref, gluw_ref)

    y_ref[...] = _merge(x_ref[...], w_in_ref, proj_ref, ybr_ref, merged_ref, wread_ref, wo_ref, ptab_ref)


def _sample_layer(layer, x, sinks, ck, cv, h0, lw):
    n = x.shape[0]
    per_layer = (ck, cv, h0, lw['w_in'], lw['a_row'], lw['bb'], lw['cc'], lw['ptab'], lw['glu_w'],
                 lw['sgu_w'], lw['w_read'], lw['w_o'])
    in_specs = [pl.BlockSpec(memory_space=pltpu.SMEM),
                pl.BlockSpec(x.shape, lambda i: (0, 0), pipeline_mode=pl.Buffered(1))]
    in_specs += [_layer_spec(layer, a.shape[1:]) for a in per_layer]
    out_shape = (jax.ShapeDtypeStruct((n, D_MODEL), F32),
                 jax.ShapeDtypeStruct((n, 2 * LANES), F32),
                 jax.ShapeDtypeStruct((n, 2 * SSM_W), F32),
                 jax.ShapeDtypeStruct((n, BRANCH_W), F32))
    out_specs = tuple(pl.BlockSpec(s.shape, lambda i: (0, 0)) for s in out_shape)
    scratch = [pltpu.VMEM((n, G0), F32), pltpu.VMEM((n, N_BRANCH * BRANCH_W), F32),
               pltpu.VMEM((n, D_MODEL), BF16)]
    return pl.pallas_call(
        functools.partial(_sample_kernel, layer),
        grid=(1,),
        in_specs=in_specs,
        out_specs=out_specs,
        out_shape=out_shape,
        scratch_shapes=scratch,
        compiler_params=pltpu.CompilerParams(
            dimension_semantics=("arbitrary",), vmem_limit_bytes=VMEM_LIMIT_BYTES),
        name="sample_layer",
    )(sinks, x, *per_layer)


def _param_table(ssm_d, glu_b, sgu_ln_g, sgu_ln_b, ln_g, ln_b, sgu_b):
    depth = ln_g.shape[0]
    rows = [jnp.concatenate([ssm_d, glu_b], axis=1),
            jnp.concatenate([sgu_ln_g, sgu_ln_b], axis=1),
            ln_g, ln_b,
            jnp.concatenate([sgu_b.reshape(depth, BRANCH_W), jnp.zeros((depth, BRANCH_W), F32)], axis=1)]
    rows += [jnp.zeros((depth, D_MODEL), F32)] * (PARAM_ROWS - len(rows))
    return jnp.stack(rows, axis=1)


def kernel(x_prompt, x_sample, cache_k_win, cache_v_win, state_ssm_re, state_ssm_im, w_in, attn_sinks,
           ssm_lambda_re, ssm_lambda_im, ssm_log_dt, ssm_b_re, ssm_b_im, ssm_c_re, ssm_c_im, ssm_d,
           glu_w, glu_b, sgu_ln_g, sgu_ln_b, sgu_w, sgu_b, w_read, w_o, ln_g, ln_b):
    nb, t, _ = x_prompt.shape
    ns = x_sample.shape[0]
    depth, _, win = cache_k_win.shape[:3]
    assert t % TB == 0 and win == WINDOW == SUB and ns % SUBLANES == 0

    a_row, bb, cc = _ssm_prep(ssm_lambda_re, ssm_lambda_im, ssm_log_dt, ssm_b_re, ssm_b_im, ssm_c_re, ssm_c_im)
    lw = dict(w_in=w_in.astype(BF16), a_row=a_row, bb=bb, cc=cc,
              ptab=_param_table(ssm_d, glu_b, sgu_ln_g, sgu_ln_b, ln_g, ln_b, sgu_b),
              glu_w=glu_w.astype(BF16), sgu_w=sgu_w, w_read=w_read.astype(BF16), w_o=w_o.astype(BF16))
    ck = cache_k_win.reshape(depth, ns, win, KV_HEADS * HEAD_DIM)
    cv = cache_v_win.reshape(depth, ns, win, KV_HEADS * HEAD_DIM)
    h0 = _state_to_kernel_layout(state_ssm_re, state_ssm_im)

    xp = x_prompt
    xs = x_sample.reshape(ns, D_MODEL)
    kp, vp, hrp, hip = [], [], [], []
    ksm, vsm, hrs, his, vcs = [], [], [], [], []
    for l in range(depth):
        xp, kvwin, hstate = _prompt_layer(l, xp, attn_sinks, lw)
        kp.append(kvwin[:, :, 0:LANES].reshape(nb, SUB, KV_HEADS, HEAD_DIM))
        vp.append(kvwin[:, :, LANES:].reshape(nb, SUB, KV_HEADS, HEAD_DIM))
        h_re, h_im = _state_from_kernel_layout(hstate[:, SUBLANES - 1, :])
        hrp.append(h_re)
        hip.append(h_im)

        xs, kv_s, h_s, vn_s = _sample_layer(l, xs, attn_sinks, ck, cv, h0, lw)
        ksm.append(kv_s[:, 0:LANES].reshape(ns, 1, KV_HEADS, HEAD_DIM))
        vsm.append(kv_s[:, LANES:].reshape(ns, 1, KV_HEADS, HEAD_DIM))
        h_re, h_im = _state_from_kernel_layout(h_s)
        hrs.append(h_re)
        his.append(h_im)
        vcs.append(vn_s.reshape(ns, 1, BRANCH_W))
    return (xp, xs.reshape(ns, 1, D_MODEL), jnp.stack(kp), jnp.stack(vp), jnp.stack(hrp), jnp.stack(hip),
            jnp.stack(ksm), jnp.stack(vsm), jnp.stack(hrs), jnp.stack(his), jnp.stack(vcs))
```

```python
import functools
import math

import jax
import jax.numpy as jnp
from jax import lax
from jax.experimental import pallas as pl
from jax.experimental.pallas import tpu as pltpu

D_MODEL = 1024
BRANCH_W = 512
HEAD_DIM = 64
N_HEADS = 8
KV_HEADS = 2
GQA_GROUP = N_HEADS // KV_HEADS
WINDOW = 128
SSM_GROUP_CH = 16
SSM_GROUPS = 32
SSM_STATE = 64
SSM_W = SSM_GROUPS * SSM_STATE
CHUNK = 128
SGU_GROUPS = 4
N_BRANCH = 3
DEPTH = 2
ALPHA = (2 * DEPTH) ** 0.25
LN_EPS = 1e-5
ATTN_SCALE = HEAD_DIM ** -0.5
NEG_INF = -1e30

Q0, K0, V0, ZA0, UB0, ZB0, UC0, VC0, ZC0, G0 = 0, 512, 640, 768, 1280, 1792, 2304, 2816, 3328, 3840
N_IN = 6912

LANES = 128
SUBLANES = 8
MXU_COLS = 256
VMEM_LIMIT_BYTES = 60 * 1024 * 1024

SUB = 128
N_SUB = 2
TB = N_SUB * SUB
PROJ_CT = MXU_COLS

SSM_TILES = BRANCH_W // LANES
SSM_TILE_GROUPS = SSM_GROUPS // SSM_TILES
SSM_HALF = SSM_W // SSM_TILES
SSM_TILE_W = 2 * SSM_HALF
SCAN_CHUNKS = SUBLANES
SCAN_LEN = SUB // SCAN_CHUNKS

ROW_DSKIP_GLUB, ROW_SGU_LN, ROW_LN_G, ROW_LN_B, ROW_SGU_B = 0, 1, 2, 3, 4
PARAM_ROWS = SUBLANES

F32 = jnp.float32
BF16 = jnp.bfloat16


def _re(j):
    return slice(j * SSM_TILE_W, j * SSM_TILE_W + SSM_HALF)


def _im(j):
    return slice(j * SSM_TILE_W + SSM_HALF, (j + 1) * SSM_TILE_W)


def _sigmoid(x):
    return 1.0 / (1.0 + jnp.exp(-x))


def _silu(x):
    return x * _sigmoid(x)


def _gelu_tanh(x):
    c = math.sqrt(2.0 / math.pi)
    return 0.5 * x * (1.0 + jnp.tanh(c * (x + 0.044715 * (x * x * x))))


def _layernorm(x, g, b):
    mu = jnp.mean(x, axis=-1, keepdims=True)
    xc = x - mu
    var = jnp.mean(xc * xc, axis=-1, keepdims=True)
    return xc * lax.rsqrt(var + LN_EPS) * g + b


def _dot(a, b):
    return jnp.dot(a, b, preferred_element_type=F32)


def _dot_nt(a, b):
    return lax.dot_general(a, b, (((1,), (1,)), ((), ())), preferred_element_type=F32)


def _slope(h):
    return 2.0 ** (-(h + 1.0))


def _dup_halves(x, lo):
    sw = pltpu.roll(x, HEAD_DIM, 1)
    return jnp.where(lo, x, sw), jnp.where(lo, sw, x)


def _ssm_prep_kernel(lr_ref, li_ref, ldt_ref, brt_ref, bit_ref, crt_ref, cit_ref,
                     arow_ref, bb_ref, cc_ref, bbs_ref, ccs_ref):
    lr = lr_ref[...]
    li = li_ref[...]
    dt = jnp.exp(ldt_ref[...])
    mag = jnp.exp(lr * dt)
    ar = mag * jnp.cos(li * dt)
    ai = mag * jnp.sin(li * dt)
    den = lr * lr + li * li
    cr = ((ar - 1.0) * lr + ai * li) / den
    ci = (ai * lr - (ar - 1.0) * li) / den
    for j in range(SSM_TILES):
        bbs_ref[...] = jnp.zeros((LANES, SSM_TILE_W), F32)
        ccs_ref[...] = jnp.zeros((SSM_TILE_W, LANES), F32)
        for gl in range(SSM_TILE_GROUPS):
            g = j * SSM_TILE_GROUPS + gl
            cols = slice(gl * SSM_STATE, (gl + 1) * SSM_STATE)
            cols_im = slice(SSM_HALF + gl * SSM_STATE, SSM_HALF + (gl + 1) * SSM_STATE)
            ch = slice(gl * SSM_GROUP_CH, (gl + 1) * SSM_GROUP_CH)
            arow_ref[:, j * SSM_TILE_W + gl * SSM_STATE:j * SSM_TILE_W + (gl + 1) * SSM_STATE] = ar[g:g + 1, :]
            arow_ref[:, j * SSM_TILE_W + SSM_HALF + gl * SSM_STATE:
                     j * SSM_TILE_W + SSM_HALF + (gl + 1) * SSM_STATE] = ai[g:g + 1, :]
            br = brt_ref[g]
            bi = bit_ref[g]
            crg = cr[g:g + 1, :]
            cig = ci[g:g + 1, :]
            bbs_ref[ch, cols] = crg * br - cig * bi
            bbs_ref[ch, cols_im] = crg * bi + cig * br
            ccs_ref[cols, ch] = crt_ref[g]
            ccs_ref[cols_im, ch] = -cit_ref[g]
        bb_ref[j] = bbs_ref[...].astype(BF16)
        cc_ref[j] = ccs_ref[...].astype(BF16)


def _ssm_prep(lam_re, lam_im, log_dt, b_re, b_im, c_re, c_im):
    depth = lam_re.shape[0]
    brt = jnp.swapaxes(b_re, 2, 3)
    bit = jnp.swapaxes(b_im, 2, 3)
    crt = jnp.swapaxes(c_re, 2, 3)
    cit = jnp.swapaxes(c_im, 2, 3)

    def spec(*shape):
        nd = len(shape)
        return pl.BlockSpec((None,) + shape, lambda l, _nd=nd: (l,) + (0,) * _nd)

    return pl.pallas_call(
        _ssm_prep_kernel,
        grid=(depth,),
        in_specs=[spec(SSM_GROUPS, SSM_STATE), spec(SSM_GROUPS, SSM_STATE), spec(SSM_GROUPS, 1),
                  spec(SSM_GROUPS, SSM_GROUP_CH, SSM_STATE), spec(SSM_GROUPS, SSM_GROUP_CH, SSM_STATE),
                  spec(SSM_GROUPS, SSM_STATE, SSM_GROUP_CH), spec(SSM_GROUPS, SSM_STATE, SSM_GROUP_CH)],
        out_specs=(spec(1, 2 * SSM_W), spec(SSM_TILES, LANES, SSM_TILE_W), spec(SSM_TILES, SSM_TILE_W, LANES)),
        out_shape=(jax.ShapeDtypeStruct((depth, 1, 2 * SSM_W), F32),
                   jax.ShapeDtypeStruct((depth, SSM_TILES, LANES, SSM_TILE_W), BF16),
                   jax.ShapeDtypeStruct((depth, SSM_TILES, SSM_TILE_W, LANES), BF16)),
        scratch_shapes=[pltpu.VMEM((LANES, SSM_TILE_W), F32), pltpu.VMEM((SSM_TILE_W, LANES), F32)],
        compiler_params=pltpu.CompilerParams(dimension_semantics=("arbitrary",)),
        name="ssm_prep",
    )(lam_re, lam_im, log_dt.reshape(depth, SSM_GROUPS, 1), brt, bit, crt, cit)


def _proj_chunk(xb, w_in_ref, proj_ref, c0):
    assert c0 + PROJ_CT <= G0
    y = _dot(xb, w_in_ref[:, c0:c0 + PROJ_CT])
    if any(z0 <= c0 < z0 + BRANCH_W for z0 in (ZA0, ZB0, ZC0)):
        y = _silu(y)
    elif c0 < K0:
        y = y * ATTN_SCALE
    proj_ref[:, c0:c0 + PROJ_CT] = y


def _s5_output(y_lin, u, ptab_ref, gluw_ref):
    y = y_lin + ptab_ref[ROW_DSKIP_GLUB:ROW_DSKIP_GLUB + 1, 0:BRANCH_W] * u
    y = _gelu_tanh(y)
    glu_b = ptab_ref[ROW_DSKIP_GLUB:ROW_DSKIP_GLUB + 1, BRANCH_W:2 * BRANCH_W]
    return y * _sigmoid(_dot(y.astype(BF16), gluw_ref[...]) + glu_b)


def _gate_chunk(xb, w_in_ref, g0):
    return _sigmoid(_dot(xb, w_in_ref[:, g0:g0 + PROJ_CT]))


def _merge(x, w_in_ref, proj_ref, ybr_ref, merged_ref, wread_ref, wo_ref, ptab_ref, gates_ref=None):
    xb = x.astype(BF16)
    ybs = [(ybr_ref[:, b * BRANCH_W:(b + 1) * BRANCH_W] * proj_ref[:, z0:z0 + BRANCH_W]).astype(BF16)
           for b, z0 in enumerate((ZA0, ZB0, ZC0))]
    for n0 in range(0, D_MODEL, PROJ_CT):
        merged = None
        for b in range(N_BRANCH):
            g0 = b * D_MODEL + n0
            gate = (_gate_chunk(xb, w_in_ref, G0 + g0) if gates_ref is None
                    else gates_ref[:, g0:g0 + PROJ_CT])
            term = gate * _dot(ybs[b], wread_ref[b, :, n0:n0 + PROJ_CT])
            merged = term if merged is None else merged + term
        merged_ref[:, n0:n0 + PROJ_CT] = merged.astype(BF16)
    out = _dot(merged_ref[...], wo_ref[...])
    return _layernorm(ALPHA * x + out, ptab_ref[ROW_LN_G:ROW_LN_G + 1, :], ptab_ref[ROW_LN_B:ROW_LN_B + 1, :])


def _sgu_layernorm(v, ptab_ref):
    return _layernorm(v, ptab_ref[ROW_SGU_LN:ROW_SGU_LN + 1, 0:BRANCH_W],
                      ptab_ref[ROW_SGU_LN:ROW_SGU_LN + 1, BRANCH_W:2 * BRANCH_W])


def _prompt_kernel(layer, sinks_ref, x_ref, w_in_ref, arow_ref, bb_ref, cc_ref, ptab_ref, gluw_ref,
                   sguw_ref, wread_ref, wo_ref,
                   y_ref, kwin_ref, vwin_ref, hre_ref, him_ref,
                   proj_ref, ybr_ref, kvprev_ref, bias_ref, atab_ref, apw_ref, hs_ref, hcarry_ref,
                   uslab_ref, yslab_ref, merged_ref, gates_ref, hsb_ref):
    bi = pl.program_id(0)
    ji = pl.program_id(1)

    @pl.when(jnp.logical_and(bi == 0, ji == 0))
    def _init_tables():
        qa = lax.broadcasted_iota(jnp.int32, (SUB, 2 * SUB), 0)
        kk = lax.broadcasted_iota(jnp.int32, (SUB, 2 * SUB), 1)
        dist = qa + WINDOW - kk
        valid = jnp.logical_and(dist >= 0, dist <= WINDOW)
        distf = dist.astype(F32)
        for h in range(N_HEADS):
            bias_ref[h] = jnp.where(valid, -_slope(h) * distf, NEG_INF)
        for j in range(SSM_TILES):
            a_r = arow_ref[:, _re(j)]
            a_i = arow_ref[:, _im(j)]
            pr, pi = a_r, a_i
            for s in range(SCAN_LEN):
                atab_ref[s, :, _re(j)] = jnp.broadcast_to(pr, (SUBLANES, SSM_HALF))
                atab_ref[s, :, _im(j)] = jnp.broadcast_to(pi, (SUBLANES, SSM_HALF))
                if s + 1 < SCAN_LEN:
                    pr, pi = pr * a_r - pi * a_i, pr * a_i + pi * a_r
            for k in range(2):
                pr, pi = pr * pr - pi * pi, 2.0 * (pr * pi)
                apw_ref[k, :, _re(j)] = jnp.broadcast_to(pr, (SUBLANES, SSM_HALF))
                apw_ref[k, :, _im(j)] = jnp.broadcast_to(pi, (SUBLANES, SSM_HALF))

    @pl.when(ji == 0)
    def _reset_carries():
        kvprev_ref[...] = jnp.zeros((SUB, 2 * LANES), F32)
        hcarry_ref[...] = jnp.zeros((SUBLANES, 2 * SSM_W), F32)

    xb = x_ref[0].astype(BF16)
    late_chunks = [z0 + c for z0 in (ZA0, ZB0, ZC0) for c in range(0, BRANCH_W, PROJ_CT)]

    def project_late(n):
        for _ in range(n):
            if late_chunks:
                _proj_chunk(xb, w_in_ref, proj_ref, late_chunks.pop(0))

    for c0 in range(UB0, UB0 + BRANCH_W, PROJ_CT):
        _proj_chunk(xb, w_in_ref, proj_ref, c0)

    for sb in range(N_SUB):
        for j in range(SSM_TILES):
            uslab_ref[sb * SSM_TILES + j] = proj_ref[sb * SUB:(sb + 1) * SUB, UB0 + j * LANES:UB0 + (j + 1) * LANES]
        for j in range(SSM_TILES):
            u_perm = jnp.concatenate(
                [uslab_ref[sb * SSM_TILES + j, pl.ds(s, SCAN_CHUNKS, stride=SCAN_LEN), :]
                 for s in range(SCAN_LEN)], axis=0)
            hs_ref[sb * SUB:(sb + 1) * SUB, j * SSM_TILE_W:(j + 1) * SSM_TILE_W] = _dot(
                u_perm.astype(BF16), bb_ref[j])

    for c0 in list(range(Q0, ZA0, PROJ_CT)) + list(range(UC0, ZC0, PROJ_CT)):
        _proj_chunk(xb, w_in_ref, proj_ref, c0)
    proj_ref[:, VC0:VC0 + BRANCH_W] = _sgu_layernorm(proj_ref[:, VC0:VC0 + BRANCH_W], ptab_ref)

    srow = lax.broadcasted_iota(jnp.int32, (SUBLANES, SSM_HALF), 0)
    for sb, j in [(sb, j) for sb in range(N_SUB) for j in range(SSM_TILES)]:
        r_base = sb * SUB
        re, im = _re(j), _im(j)
        a_r = atab_ref[0, :, re]
        a_i = atab_ref[0, :, im]
        project_late(1)
        c_r = jnp.broadcast_to(hcarry_ref[SUBLANES - 1:SUBLANES, re], (SUBLANES, SSM_HALF))
        c_i = jnp.broadcast_to(hcarry_ref[SUBLANES - 1:SUBLANES, im], (SUBLANES, SSM_HALF))
        h_r = hs_ref[r_base:r_base + SUBLANES, re] + jnp.where(srow == 0, a_r * c_r - a_i * c_i, 0.0)
        h_i = hs_ref[r_base:r_base + SUBLANES, im] + jnp.where(srow == 0, a_r * c_i + a_i * c_r, 0.0)
        hs_ref[r_base:r_base + SUBLANES, re] = h_r
        hs_ref[r_base:r_base + SUBLANES, im] = h_i
        for s in range(1, SCAN_LEN):
            rows = slice(r_base + s * SUBLANES, r_base + (s + 1) * SUBLANES)
            h_r, h_i = (hs_ref[rows, re] + (a_r * h_r - a_i * h_i),
                        hs_ref[rows, im] + (a_r * h_i + a_i * h_r))
            hs_ref[rows, re] = h_r
            hs_ref[rows, im] = h_i
        e_r, e_i = h_r, h_i
        for k, d in enumerate((1, 2, 4)):
            if k == 0:
                p_r, p_i = atab_ref[SCAN_LEN - 1, :, re], atab_ref[SCAN_LEN - 1, :, im]
            else:
                p_r, p_i = apw_ref[k - 1, :, re], apw_ref[k - 1, :, im]
            s_r = pltpu.roll(e_r, d, 0)
            s_i = pltpu.roll(e_i, d, 0)
            e_r, e_i = (e_r + jnp.where(srow >= d, p_r * s_r - p_i * s_i, 0.0),
                        e_i + jnp.where(srow >= d, p_r * s_i + p_i * s_r, 0.0))
        hcarry_ref[:, re] = e_r
        hcarry_ref[:, im] = e_i
        hre_ref[0, :, j * SSM_HALF:(j + 1) * SSM_HALF] = e_r
        him_ref[0, :, j * SSM_HALF:(j + 1) * SSM_HALF] = e_i
        in_r = jnp.where(srow == 0, 0.0, pltpu.roll(e_r, 1, 0))
        in_i = jnp.where(srow == 0, 0.0, pltpu.roll(e_i, 1, 0))
        in_r2 = jnp.concatenate([in_r, in_r], axis=0)
        in_i2 = jnp.concatenate([in_i, in_i], axis=0)
        for s in range(0, SCAN_LEN, 2):
            rows = slice(r_base + s * SUBLANES, r_base + (s + 2) * SUBLANES)
            t_r = jnp.concatenate([atab_ref[s, :, re], atab_ref[s + 1, :, re]], axis=0)
            t_i = jnp.concatenate([atab_ref[s, :, im], atab_ref[s + 1, :, im]], axis=0)
            hsb_ref[rows, re] = (hs_ref[rows, re] + (t_r * in_r2 - t_i * in_i2)).astype(BF16)
            hsb_ref[rows, im] = (hs_ref[rows, im] + (t_r * in_i2 + t_i * in_r2)).astype(BF16)
    assert not late_chunks
    for sb in range(N_SUB):
        for j in range(SSM_TILES):
            y_perm = _dot(hsb_ref[sb * SUB:(sb + 1) * SUB, j * SSM_TILE_W:(j + 1) * SSM_TILE_W],
                          cc_ref[j])
            for s in range(SCAN_LEN):
                yslab_ref[sb * SSM_TILES + j, pl.ds(s, SCAN_CHUNKS, stride=SCAN_LEN), :] = (
                    y_perm[s * SUBLANES:(s + 1) * SUBLANES, :])
    y_lin = jnp.concatenate(
        [jnp.concatenate([yslab_ref[sb * SSM_TILES + j] for j in range(SSM_TILES)], axis=1)
         for sb in range(N_SUB)], axis=0)
    ybr_ref[:, BRANCH_W:2 * BRANCH_W] = _s5_output(y_lin, proj_ref[:, UB0:UB0 + BRANCH_W], ptab_ref, gluw_ref)

    lane2 = lax.broadcasted_iota(jnp.int32, (2 * SUB, LANES), 1)
    kcol = lax.broadcasted_iota(jnp.int32, (SUB, 2 * SUB), 1)
    lane = lax.broadcasted_iota(jnp.int32, (SUB, LANES), 1)
    lo = lane < HEAD_DIM
    first_neg = jnp.where(ji == 0, NEG_INF, 0.0).astype(F32)
    kv_prev = kvprev_ref[...]
    gate_chunks = list(range(0, N_BRANCH * D_MODEL, PROJ_CT))
    per_tile = -(-len(gate_chunks) // (N_SUB * N_HEADS // 2))

    def project_gates(n):
        for _ in range(n):
            if gate_chunks:
                g0 = gate_chunks.pop(0)
                gates_ref[:, g0:g0 + PROJ_CT] = _gate_chunk(xb, w_in_ref, G0 + g0)

    for sb in range(N_SUB):
        rows = slice(sb * SUB, (sb + 1) * SUB)
        kv_cur = proj_ref[rows, K0:K0 + 2 * LANES]
        k_dup = _dup_halves(jnp.concatenate([kv_prev[:, 0:LANES], kv_cur[:, 0:LANES]], axis=0), lane2 < HEAD_DIM)
        v_dup = _dup_halves(jnp.concatenate([kv_prev[:, LANES:], kv_cur[:, LANES:]], axis=0), lane2 < HEAD_DIM)
        k_dup = [k.astype(BF16) for k in k_dup]
        v_dup = [v.astype(BF16) for v in v_dup]
        for t in range(N_HEADS // 2):
            project_gates(per_tile)
            qt = proj_ref[rows, Q0 + t * LANES:Q0 + (t + 1) * LANES]
            kv_head = (2 * t) // GQA_GROUP
            halves = []
            for half in range(2):
                h = 2 * t + half
                qm = jnp.where(lo if half == 0 else jnp.logical_not(lo), qt, 0.0).astype(BF16)
                s = _dot_nt(qm, k_dup[kv_head]) + bias_ref[h]
                if sb == 0:
                    s = s + jnp.where(kcol < SUB, first_neg, 0.0)
                sink = sinks_ref[layer, h]
                m = jnp.maximum(jnp.max(s, axis=-1, keepdims=True), sink)
                p = jnp.exp(s - m)
                denom = jnp.sum(p, axis=-1, keepdims=True) + jnp.exp(sink - m)
                o = _dot(p.astype(BF16), v_dup[kv_head])
                halves.append(o * (1.0 / denom))
            ybr_ref[rows, t * LANES:(t + 1) * LANES] = jnp.where(lo, halves[0], halves[1])
        kv_prev = kv_cur
    kvprev_ref[...] = kv_prev
    kwin_ref[0] = kv_prev[:, 0:LANES]
    vwin_ref[0] = kv_prev[:, LANES:]
    project_gates(len(gate_chunks))

    trow = lax.broadcasted_iota(jnp.int32, (CHUNK, CHUNK), 0)
    tcol = lax.broadcasted_iota(jnp.int32, (CHUNK, CHUNK), 1)
    tril = trow >= tcol
    for g in range(SGU_GROUPS):
        wm = jnp.where(tril, sguw_ref[g], 0.0).astype(BF16)
        b_row = ptab_ref[ROW_SGU_B:ROW_SGU_B + 1, g * CHUNK:(g + 1) * CHUNK]
        b_col = jnp.sum(jnp.where(trow == tcol, b_row, 0.0), axis=1, keepdims=True)
        for sb in range(N_SUB):
            rows = slice(sb * SUB, (sb + 1) * SUB)
            vn_g = proj_ref[rows, VC0 + g * LANES:VC0 + (g + 1) * LANES]
            sg = _dot(wm, vn_g.astype(BF16)) + b_col
            ybr_ref[rows, 2 * BRANCH_W + g * LANES:2 * BRANCH_W + (g + 1) * LANES] = (
                proj_ref[rows, UC0 + g * LANES:UC0 + (g + 1) * LANES] * sg)

    y_ref[0] = _merge(x_ref[0], w_in_ref, proj_ref, ybr_ref, merged_ref, wread_ref, wo_ref, ptab_ref, gates_ref)


def _layer_spec(layer, shape):
    nd = len(shape)
    return pl.BlockSpec((None,) + tuple(shape), lambda *_, _nd=nd: (layer,) + (0,) * _nd,
                        pipeline_mode=pl.Buffered(1))


def _prompt_layer(layer, x, sinks, lw):
    nb, t, _ = x.shape
    grid = (nb, t // TB)
    weights = (lw['w_in'], lw['a_row'], lw['bb'], lw['cc'], lw['ptab'], lw['glu_w'], lw['sgu_w'],
               lw['w_read'], lw['w_o'])
    in_specs = [pl.BlockSpec(memory_space=pltpu.SMEM),
                pl.BlockSpec((1, TB, D_MODEL), lambda b, j: (b, j, 0))]
    in_specs += [_layer_spec(layer, w.shape[1:]) for w in weights]
    out_shape = (jax.ShapeDtypeStruct((nb, t, D_MODEL), F32),
                 jax.ShapeDtypeStruct((nb, SUB, LANES), F32),
                 jax.ShapeDtypeStruct((nb, SUB, LANES), F32),
                 jax.ShapeDtypeStruct((nb, SUBLANES, SSM_W), F32),
                 jax.ShapeDtypeStruct((nb, SUBLANES, SSM_W), F32))
    out_specs = (pl.BlockSpec((1, TB, D_MODEL), lambda b, j: (b, j, 0)),
                 pl.BlockSpec((1, SUB, LANES), lambda b, j: (b, 0, 0)),
                 pl.BlockSpec((1, SUB, LANES), lambda b, j: (b, 0, 0)),
                 pl.BlockSpec((1, SUBLANES, SSM_W), lambda b, j: (b, 0, 0)),
                 pl.BlockSpec((1, SUBLANES, SSM_W), lambda b, j: (b, 0, 0)))
    scratch = [pltpu.VMEM((TB, G0), F32),
               pltpu.VMEM((TB, N_BRANCH * BRANCH_W), F32),
               pltpu.VMEM((SUB, 2 * LANES), F32),
               pltpu.VMEM((N_HEADS, SUB, 2 * SUB), F32),
               pltpu.VMEM((SCAN_LEN, SUBLANES, 2 * SSM_W), F32),
               pltpu.VMEM((2, SUBLANES, 2 * SSM_W), F32),
               pltpu.VMEM((TB, 2 * SSM_W), F32),
               pltpu.VMEM((SUBLANES, 2 * SSM_W), F32),
               pltpu.VMEM((N_SUB * SSM_TILES, SUB, LANES), F32),
               pltpu.VMEM((N_SUB * SSM_TILES, SUB, LANES), F32),
               pltpu.VMEM((TB, D_MODEL), BF16),
               pltpu.VMEM((TB, N_BRANCH * D_MODEL), F32),
               pltpu.VMEM((TB, 2 * SSM_W), BF16)]
    return pl.pallas_call(
        functools.partial(_prompt_kernel, layer),
        grid=grid,
        in_specs=in_specs,
        out_specs=out_specs,
        out_shape=out_shape,
        scratch_shapes=scratch,
        compiler_params=pltpu.CompilerParams(
            dimension_semantics=("arbitrary", "arbitrary"),
            vmem_limit_bytes=VMEM_LIMIT_BYTES),
        name="prompt_layer",
    )(sinks, x, *weights)


def _sample_kernel(layer, sinks_ref, x_ref, ck_ref, cv_ref, h0re_ref, h0im_ref, w_in_ref, arow_ref, bb_ref, cc_ref,
                   ptab_ref, gluw_ref, sguw_ref, wread_ref, wo_ref,
                   y_ref, kout_ref, vout_ref, hre_ref, him_ref, vn_ref,
                   proj_ref, ybr_ref, merged_ref):
    n_seq = x_ref.shape[0]
    win = ck_ref.shape[1]
    xb = x_ref[...].astype(BF16)
    for c0 in range(0, G0, PROJ_CT):
        _proj_chunk(xb, w_in_ref, proj_ref, c0)
    kout_ref[...] = proj_ref[:, K0:K0 + LANES]
    vout_ref[...] = proj_ref[:, V0:V0 + LANES]

    row = lax.broadcasted_iota(jnp.int32, (SUBLANES, LANES), 0)
    lane = lax.broadcasted_iota(jnp.int32, (SUBLANES, LANES), 1)
    lane_lo = lane < HEAD_DIM
    row_even = (row % 2) == 0
    sel = jnp.logical_or(jnp.logical_and(row_even, lane_lo),
                         jnp.logical_and(jnp.logical_not(row_even), jnp.logical_not(lane_lo)))
    swap = (row // GQA_GROUP) != (row % 2)
    slope = jnp.exp2(-(row.astype(F32) + 1.0))
    bias = -slope * (win - lane).astype(F32)
    sink = jnp.zeros((SUBLANES, 1), F32)
    for h in range(N_HEADS):
        sink = jnp.where(row[:, 0:1] == h, sinks_ref[layer, h], sink)

    def seq_group_step(gi, carry):
        r0 = pl.multiple_of(gi * SUBLANES, SUBLANES)
        q_rows = proj_ref[pl.ds(r0, SUBLANES), Q0:Q0 + BRANCH_W]
        k_rows = proj_ref[pl.ds(r0, SUBLANES), K0:K0 + LANES]
        v_rows = proj_ref[pl.ds(r0, SUBLANES), V0:V0 + LANES]
        q8s, scores = [], []
        for r in range(SUBLANES):
            q8 = jnp.zeros((SUBLANES, LANES), F32)
            for t in range(N_HEADS // 2):
                qt = jnp.broadcast_to(q_rows[r:r + 1, t * LANES:(t + 1) * LANES], (SUBLANES, LANES))
                q8 = jnp.where(jnp.logical_and(sel, row // 2 == t), qt, q8)
            q8 = jnp.where(swap, pltpu.roll(q8, HEAD_DIM, 1), q8)
            q8s.append(q8)
            scores.append(_dot_nt(q8.astype(BF16), ck_ref[r0 + r].astype(BF16)) + bias)
        s = jnp.concatenate(scores, axis=0)
        k_own = jnp.concatenate([jnp.broadcast_to(k_rows[r:r + 1, :], (SUBLANES, LANES))
                                 for r in range(SUBLANES)], axis=0)
        s_own = jnp.sum(jnp.concatenate(q8s, axis=0) * k_own, axis=-1, keepdims=True)
        sink_g = jnp.concatenate([sink] * SUBLANES, axis=0)
        m = jnp.maximum(jnp.maximum(jnp.max(s, axis=-1, keepdims=True), s_own), sink_g)
        p = jnp.exp(s - m)
        p_own = jnp.exp(s_own - m)
        inv = 1.0 / (jnp.sum(p, axis=-1, keepdims=True) + p_own + jnp.exp(sink_g - m))
        outs = [[] for _ in range(N_HEADS // 2)]
        for r in range(SUBLANES):
            rows8 = slice(r * SUBLANES, (r + 1) * SUBLANES)
            o = (_dot(p[rows8, :].astype(BF16), cv_ref[r0 + r].astype(BF16))
                 + p_own[rows8, :] * v_rows[r:r + 1, :]) * inv[rows8, :]
            o = jnp.where(swap, pltpu.roll(o, HEAD_DIM, 1), o)
            o = jnp.where(sel, o, 0.0)
            for t in range(N_HEADS // 2):
                outs[t].append(o[2 * t:2 * t + 1, :] + o[2 * t + 1:2 * t + 2, :])
        for t in range(N_HEADS // 2):
            ybr_ref[pl.ds(r0, SUBLANES), t * LANES:(t + 1) * LANES] = jnp.concatenate(outs[t], axis=0)
        return carry

    lax.fori_loop(0, n_seq // SUBLANES, seq_group_step, 0, unroll=2)

    vn = _sgu_layernorm(proj_ref[:, VC0:VC0 + BRANCH_W], ptab_ref)
    vn_ref[...] = vn
    for g in range(SGU_GROUPS):
        w00 = sguw_ref[g, 0:1, 0:1]
        b0 = ptab_ref[ROW_SGU_B:ROW_SGU_B + 1, g * CHUNK:g * CHUNK + 1]
        cols = slice(g * LANES, (g + 1) * LANES)
        ybr_ref[:, 2 * BRANCH_W + g * LANES:2 * BRANCH_W + (g + 1) * LANES] = (
            proj_ref[:, UC0 + g * LANES:UC0 + (g + 1) * LANES] * (w00 * vn[:, cols] + b0))

    u = proj_ref[:, UB0:UB0 + BRANCH_W]
    y_tiles = []
    for j in range(SSM_TILES):
        re, im = _re(j), _im(j)
        xs = _dot(u[:, j * LANES:(j + 1) * LANES].astype(BF16), bb_ref[j])
        a_r = arow_ref[:, re]
        a_i = arow_ref[:, im]
        natural = slice(j * SSM_HALF, (j + 1) * SSM_HALF)
        h0r = h0re_ref[:, natural]
        h0i = h0im_ref[:, natural]
        hr = xs[:, 0:SSM_HALF] + (a_r * h0r - a_i * h0i)
        hi = xs[:, SSM_HALF:] + (a_r * h0i + a_i * h0r)
        hre_ref[:, natural] = hr
        him_ref[:, natural] = hi
        y_tiles.append(_dot(jnp.concatenate([hr, hi], axis=1).astype(BF16), cc_ref[j]))
    y_lin = jnp.concatenate(y_tiles, axis=1)
    ybr_ref[:, BRANCH_W:2 * BRANCH_W] = _s5_output(y_lin, u, ptab_ref, gluw_ref)

    y_ref[...] = _merge(x_ref[...], w_in_ref, proj_ref, ybr_ref, merged_ref, wread_ref, wo_ref, ptab_ref)


def _sample_layer(layer, x, sinks, ck, cv, h0_re, h0_im, lw):
    n = x.shape[0]
    per_layer = (ck, cv, h0_re, h0_im, lw['w_in'], lw['a_row'], lw['bb'], lw['cc'], lw['ptab'], lw['glu_w'],
                 lw['sgu_w'], lw['w_read'], lw['w_o'])
    in_specs = [pl.BlockSpec(memory_space=pltpu.SMEM),
                pl.BlockSpec(x.shape, lambda i: (0, 0), pipeline_mode=pl.Buffered(1))]
    in_specs += [_layer_spec(layer, a.shape[1:]) for a in per_layer]
    out_shape = (jax.ShapeDtypeStruct((n, D_MODEL), F32),
                 jax.ShapeDtypeStruct((n, LANES), F32),
                 jax.ShapeDtypeStruct((n, LANES), F32),
                 jax.ShapeDtypeStruct((n, SSM_W), F32),
                 jax.ShapeDtypeStruct((n, SSM_W), F32),
                 jax.ShapeDtypeStruct((n, BRANCH_W), F32))
    out_specs = tuple(pl.BlockSpec(s.shape, lambda i: (0, 0)) for s in out_shape)
    scratch = [pltpu.VMEM((n, G0), F32), pltpu.VMEM((n, N_BRANCH * BRANCH_W), F32),
               pltpu.VMEM((n, D_MODEL), BF16)]
    return pl.pallas_call(
        functools.partial(_sample_kernel, layer),
        grid=(1,),
        in_specs=in_specs,
        out_specs=out_specs,
        out_shape=out_shape,
        scratch_shapes=scratch,
        compiler_params=pltpu.CompilerParams(
            dimension_semantics=("arbitrary",), vmem_limit_bytes=VMEM_LIMIT_BYTES),
        name="sample_layer",
    )(sinks, x, *per_layer)


def _param_table(ssm_d, glu_b, sgu_ln_g, sgu_ln_b, ln_g, ln_b, sgu_b):
    depth = ln_g.shape[0]
    rows = [jnp.concatenate([ssm_d, glu_b], axis=1),
            jnp.concatenate([sgu_ln_g, sgu_ln_b], axis=1),
            ln_g, ln_b,
            jnp.concatenate([sgu_b.reshape(depth, BRANCH_W), jnp.zeros((depth, BRANCH_W), F32)], axis=1)]
    rows += [jnp.zeros((depth, D_MODEL), F32)] * (PARAM_ROWS - len(rows))
    return jnp.stack(rows, axis=1)


def kernel(x_prompt, x_sample, cache_k_win, cache_v_win, state_ssm_re, state_ssm_im, w_in, attn_sinks,
           ssm_lambda_re, ssm_lambda_im, ssm_log_dt, ssm_b_re, ssm_b_im, ssm_c_re, ssm_c_im, ssm_d,
           glu_w, glu_b, sgu_ln_g, sgu_ln_b, sgu_w, sgu_b, w_read, w_o, ln_g, ln_b):
    nb, t, _ = x_prompt.shape
    ns = x_sample.shape[0]
    depth, _, win = cache_k_win.shape[:3]
    assert t % TB == 0 and win == WINDOW == SUB and ns % SUBLANES == 0

    a_row, bb, cc = _ssm_prep(ssm_lambda_re, ssm_lambda_im, ssm_log_dt, ssm_b_re, ssm_b_im, ssm_c_re, ssm_c_im)
    lw = dict(w_in=w_in.astype(BF16), a_row=a_row, bb=bb, cc=cc,
              ptab=_param_table(ssm_d, glu_b, sgu_ln_g, sgu_ln_b, ln_g, ln_b, sgu_b),
              glu_w=glu_w.astype(BF16), sgu_w=sgu_w, w_read=w_read.astype(BF16), w_o=w_o.astype(BF16))
    ck = cache_k_win.reshape(depth, ns, win, KV_HEADS * HEAD_DIM)
    cv = cache_v_win.reshape(depth, ns, win, KV_HEADS * HEAD_DIM)
    h0_re = state_ssm_re.reshape(depth, ns, SSM_W)
    h0_im = state_ssm_im.reshape(depth, ns, SSM_W)

    xp = x_prompt
    xs = x_sample.reshape(ns, D_MODEL)
    kp, vp, hrp, hip = [], [], [], []
    ksm, vsm, hrs, his, vcs = [], [], [], [], []
    for l in range(depth):
        xp, kwin, vwin, hre, him = _prompt_layer(l, xp, attn_sinks, lw)
        kp.append(kwin.reshape(nb, SUB, KV_HEADS, HEAD_DIM))
        vp.append(vwin.reshape(nb, SUB, KV_HEADS, HEAD_DIM))
        hrp.append(hre[:, SUBLANES - 1, :].reshape(nb, SSM_GROUPS, SSM_STATE))
        hip.append(him[:, SUBLANES - 1, :].reshape(nb, SSM_GROUPS, SSM_STATE))

        xs, k_s, v_s, hre_s, him_s, vn_s = _sample_layer(l, xs, attn_sinks, ck, cv, h0_re, h0_im, lw)
        ksm.append(k_s.reshape(ns, 1, KV_HEADS, HEAD_DIM))
        vsm.append(v_s.reshape(ns, 1, KV_HEADS, HEAD_DIM))
        hrs.append(hre_s.reshape(ns, SSM_GROUPS, SSM_STATE))
        his.append(him_s.reshape(ns, SSM_GROUPS, SSM_STATE))
        vcs.append(vn_s.reshape(ns, 1, BRANCH_W))
    return (xp, xs.reshape(ns, 1, D_MODEL), jnp.stack(kp), jnp.stack(vp), jnp.stack(hrp), jnp.stack(hip),
            jnp.stack(ksm), jnp.stack(vsm), jnp.stack(hrs), jnp.stack(his), jnp.stack(vcs))
```

```python
import functools
import math

import jax
import jax.numpy as jnp
from jax import lax
from jax.experimental import pallas as pl
from jax.experimental.pallas import tpu as pltpu

D_MODEL = 1024
BRANCH_W = 512
HEAD_DIM = 64
N_HEADS = 8
KV_HEADS = 2
GQA_GROUP = N_HEADS // KV_HEADS
WINDOW = 128
SSM_GROUP_CH = 16
SSM_GROUPS = 32
SSM_STATE = 64
SSM_W = SSM_GROUPS * SSM_STATE
CHUNK = 128
SGU_GROUPS = 4
N_BRANCH = 3
DEPTH = 2
ALPHA = (2 * DEPTH) ** 0.25
LN_EPS = 1e-5
ATTN_SCALE = HEAD_DIM ** -0.5
NEG_INF = -1e30

Q0, K0, V0, ZA0, UB0, ZB0, UC0, VC0, ZC0, G0 = 0, 512, 640, 768, 1280, 1792, 2304, 2816, 3328, 3840
N_IN = 6912

LANES = 128
SUBLANES = 8
MXU_COLS = 256
VMEM_LIMIT_BYTES = 60 * 1024 * 1024

SUB = 128
N_SUB = 2
TB = N_SUB * SUB
PROJ_CT = MXU_COLS

SSM_TILES = BRANCH_W // LANES
SSM_TILE_GROUPS = SSM_GROUPS // SSM_TILES
SSM_HALF = SSM_W // SSM_TILES
SSM_TILE_W = 2 * SSM_HALF
SCAN_CHUNKS = SUBLANES
SCAN_LEN = SUB // SCAN_CHUNKS

ROW_DSKIP_GLUB, ROW_SGU_LN, ROW_LN_G, ROW_LN_B, ROW_SGU_B = 0, 1, 2, 3, 4
PARAM_ROWS = SUBLANES

F32 = jnp.float32
BF16 = jnp.bfloat16


def _re(j):
    return slice(j * SSM_TILE_W, j * SSM_TILE_W + SSM_HALF)


def _im(j):
    return slice(j * SSM_TILE_W + SSM_HALF, (j + 1) * SSM_TILE_W)


def _sigmoid(x):
    return 1.0 / (1.0 + jnp.exp(-x))


def _silu(x):
    return x * _sigmoid(x)


def _gelu_tanh(x):
    c = math.sqrt(2.0 / math.pi)
    return 0.5 * x * (1.0 + jnp.tanh(c * (x + 0.044715 * (x * x * x))))


def _layernorm(x, g, b):
    mu = jnp.mean(x, axis=-1, keepdims=True)
    xc = x - mu
    var = jnp.mean(xc * xc, axis=-1, keepdims=True)
    return xc * lax.rsqrt(var + LN_EPS) * g + b


def _dot(a, b):
    return jnp.dot(a, b, preferred_element_type=F32)


def _dot_nt(a, b):
    return lax.dot_general(a, b, (((1,), (1,)), ((), ())), preferred_element_type=F32)


def _slope(h):
    return 2.0 ** (-(h + 1.0))


def _dup_halves(x, lo):
    sw = pltpu.roll(x, HEAD_DIM, 1)
    return jnp.where(lo, x, sw), jnp.where(lo, sw, x)


def _ssm_prep_kernel(lr_ref, li_ref, ldt_ref, brt_ref, bit_ref, crt_ref, cit_ref,
                     arow_ref, bb_ref, cc_ref, bbs_ref, ccs_ref):
    lr = lr_ref[...]
    li = li_ref[...]
    dt = jnp.exp(ldt_ref[...])
    mag = jnp.exp(lr * dt)
    ar = mag * jnp.cos(li * dt)
    ai = mag * jnp.sin(li * dt)
    den = lr * lr + li * li
    cr = ((ar - 1.0) * lr + ai * li) / den
    ci = (ai * lr - (ar - 1.0) * li) / den
    for j in range(SSM_TILES):
        bbs_ref[...] = jnp.zeros((LANES, SSM_TILE_W), F32)
        ccs_ref[...] = jnp.zeros((SSM_TILE_W, LANES), F32)
        for gl in range(SSM_TILE_GROUPS):
            g = j * SSM_TILE_GROUPS + gl
            cols = slice(gl * SSM_STATE, (gl + 1) * SSM_STATE)
            cols_im = slice(SSM_HALF + gl * SSM_STATE, SSM_HALF + (gl + 1) * SSM_STATE)
            ch = slice(gl * SSM_GROUP_CH, (gl + 1) * SSM_GROUP_CH)
            arow_ref[:, j * SSM_TILE_W + gl * SSM_STATE:j * SSM_TILE_W + (gl + 1) * SSM_STATE] = ar[g:g + 1, :]
            arow_ref[:, j * SSM_TILE_W + SSM_HALF + gl * SSM_STATE:
                     j * SSM_TILE_W + SSM_HALF + (gl + 1) * SSM_STATE] = ai[g:g + 1, :]
            br = brt_ref[g]
            bi = bit_ref[g]
            crg = cr[g:g + 1, :]
            cig = ci[g:g + 1, :]
            bbs_ref[ch, cols] = crg * br - cig * bi
            bbs_ref[ch, cols_im] = crg * bi + cig * br
            ccs_ref[cols, ch] = crt_ref[g]
            ccs_ref[cols_im, ch] = -cit_ref[g]
        bb_ref[j] = bbs_ref[...].astype(BF16)
        cc_ref[j] = ccs_ref[...].astype(BF16)


def _ssm_prep(lam_re, lam_im, log_dt, b_re, b_im, c_re, c_im):
    depth = lam_re.shape[0]
    brt = jnp.swapaxes(b_re, 2, 3)
    bit = jnp.swapaxes(b_im, 2, 3)
    crt = jnp.swapaxes(c_re, 2, 3)
    cit = jnp.swapaxes(c_im, 2, 3)

    def spec(*shape):
        nd = len(shape)
        return pl.BlockSpec((None,) + shape, lambda l, _nd=nd: (l,) + (0,) * _nd)

    return pl.pallas_call(
        _ssm_prep_kernel,
        grid=(depth,),
        in_specs=[spec(SSM_GROUPS, SSM_STATE), spec(SSM_GROUPS, SSM_STATE), spec(SSM_GROUPS, 1),
                  spec(SSM_GROUPS, SSM_GROUP_CH, SSM_STATE), spec(SSM_GROUPS, SSM_GROUP_CH, SSM_STATE),
                  spec(SSM_GROUPS, SSM_STATE, SSM_GROUP_CH), spec(SSM_GROUPS, SSM_STATE, SSM_GROUP_CH)],
        out_specs=(spec(1, 2 * SSM_W), spec(SSM_TILES, LANES, SSM_TILE_W), spec(SSM_TILES, SSM_TILE_W, LANES)),
        out_shape=(jax.ShapeDtypeStruct((depth, 1, 2 * SSM_W), F32),
                   jax.ShapeDtypeStruct((depth, SSM_TILES, LANES, SSM_TILE_W), BF16),
                   jax.ShapeDtypeStruct((depth, SSM_TILES, SSM_TILE_W, LANES), BF16)),
        scratch_shapes=[pltpu.VMEM((LANES, SSM_TILE_W), F32), pltpu.VMEM((SSM_TILE_W, LANES), F32)],
        compiler_params=pltpu.CompilerParams(dimension_semantics=("arbitrary",)),
        name="ssm_prep",
    )(lam_re, lam_im, log_dt.reshape(depth, SSM_GROUPS, 1), brt, bit, crt, cit)


def _proj_chunk(xb, w_in_ref, proj_ref, c0):
    assert c0 + PROJ_CT <= G0
    y = _dot(xb, w_in_ref[:, c0:c0 + PROJ_CT])
    if any(z0 <= c0 < z0 + BRANCH_W for z0 in (ZA0, ZB0, ZC0)):
        y = _silu(y)
    elif c0 < K0:
        y = y * ATTN_SCALE
    proj_ref[:, c0:c0 + PROJ_CT] = y


def _s5_output(y_lin, u, ptab_ref, gluw_ref):
    y = y_lin + ptab_ref[ROW_DSKIP_GLUB:ROW_DSKIP_GLUB + 1, 0:BRANCH_W] * u
    y = _gelu_tanh(y)
    glu_b = ptab_ref[ROW_DSKIP_GLUB:ROW_DSKIP_GLUB + 1, BRANCH_W:2 * BRANCH_W]
    return y * _sigmoid(_dot(y.astype(BF16), gluw_ref[...]) + glu_b)


def _gate_chunk(xb, w_in_ref, g0):
    return _sigmoid(_dot(xb, w_in_ref[:, g0:g0 + PROJ_CT]))


def _merge(x, w_in_ref, proj_ref, ybr_ref, merged_ref, wread_ref, wo_ref, ptab_ref, gates_ref=None):
    xb = x.astype(BF16)
    ybs = [(ybr_ref[:, b * BRANCH_W:(b + 1) * BRANCH_W] * proj_ref[:, z0:z0 + BRANCH_W]).astype(BF16)
           for b, z0 in enumerate((ZA0, ZB0, ZC0))]
    for n0 in range(0, D_MODEL, PROJ_CT):
        merged = None
        for b in range(N_BRANCH):
            g0 = b * D_MODEL + n0
            gate = (_gate_chunk(xb, w_in_ref, G0 + g0) if gates_ref is None
                    else gates_ref[:, g0:g0 + PROJ_CT])
            term = gate * _dot(ybs[b], wread_ref[b, :, n0:n0 + PROJ_CT])
            merged = term if merged is None else merged + term
        merged_ref[:, n0:n0 + PROJ_CT] = merged.astype(BF16)
    out = _dot(merged_ref[...], wo_ref[...])
    return _layernorm(ALPHA * x + out, ptab_ref[ROW_LN_G:ROW_LN_G + 1, :], ptab_ref[ROW_LN_B:ROW_LN_B + 1, :])


def _sgu_layernorm(v, ptab_ref):
    return _layernorm(v, ptab_ref[ROW_SGU_LN:ROW_SGU_LN + 1, 0:BRANCH_W],
                      ptab_ref[ROW_SGU_LN:ROW_SGU_LN + 1, BRANCH_W:2 * BRANCH_W])


def _prompt_kernel(layer, sinks_ref, x_ref, w_in_ref, arow_ref, bb_ref, cc_ref, ptab_ref, gluw_ref,
                   sguw_ref, wread_ref, wo_ref,
                   y_ref, kwin_ref, vwin_ref, hre_ref, him_ref,
                   proj_ref, ybr_ref, kvprev_ref, bias_ref, atab_ref, apw_ref, pchunk_ref, hs_ref, hcarry_ref,
                   uslab_ref, yslab_ref, merged_ref, gates_ref, hsb_ref):
    bi = pl.program_id(0)
    ji = pl.program_id(1)

    @pl.when(jnp.logical_and(bi == 0, ji == 0))
    def _init_tables():
        qa = lax.broadcasted_iota(jnp.int32, (SUB, 2 * SUB), 0)
        kk = lax.broadcasted_iota(jnp.int32, (SUB, 2 * SUB), 1)
        dist = qa + WINDOW - kk
        valid = jnp.logical_and(dist >= 0, dist <= WINDOW)
        distf = dist.astype(F32)
        for h in range(N_HEADS):
            bias_ref[h] = jnp.where(valid, -_slope(h) * distf, NEG_INF)
        for j in range(SSM_TILES):
            a_r = arow_ref[:, _re(j)]
            a_i = arow_ref[:, _im(j)]
            pr, pi = a_r, a_i
            for s in range(SCAN_LEN):
                atab_ref[s, :, _re(j)] = jnp.broadcast_to(pr, (SUBLANES, SSM_HALF))
                atab_ref[s, :, _im(j)] = jnp.broadcast_to(pi, (SUBLANES, SSM_HALF))
                if s + 1 < SCAN_LEN:
                    pr, pi = pr * a_r - pi * a_i, pr * a_i + pi * a_r
            q_r, q_i = pr, pi
            rows_r, rows_i = [q_r], [q_i]
            for _ in range(SCAN_CHUNKS - 1):
                rows_r.append(rows_r[-1] * q_r - rows_i[-1] * q_i)
                rows_i.append(rows_r[-2] * q_i + rows_i[-1] * q_r)
            pchunk_ref[:, _re(j)] = jnp.concatenate(rows_r, axis=0)
            pchunk_ref[:, _im(j)] = jnp.concatenate(rows_i, axis=0)
            for k in range(2):
                pr, pi = pr * pr - pi * pi, 2.0 * (pr * pi)
                apw_ref[k, :, _re(j)] = jnp.broadcast_to(pr, (SUBLANES, SSM_HALF))
                apw_ref[k, :, _im(j)] = jnp.broadcast_to(pi, (SUBLANES, SSM_HALF))

    @pl.when(ji == 0)
    def _reset_carries():
        kvprev_ref[...] = jnp.zeros((SUB, 2 * LANES), F32)
        hcarry_ref[...] = jnp.zeros((SUBLANES, 2 * SSM_W), F32)

    xb = x_ref[0].astype(BF16)
    late_chunks = [z0 + c for z0 in (ZA0, ZB0, ZC0) for c in range(0, BRANCH_W, PROJ_CT)]

    def project_late(n):
        for _ in range(n):
            if late_chunks:
                _proj_chunk(xb, w_in_ref, proj_ref, late_chunks.pop(0))

    for c0 in range(UB0, UB0 + BRANCH_W, PROJ_CT):
        _proj_chunk(xb, w_in_ref, proj_ref, c0)

    for sb in range(N_SUB):
        for j in range(SSM_TILES):
            uslab_ref[sb * SSM_TILES + j] = proj_ref[sb * SUB:(sb + 1) * SUB, UB0 + j * LANES:UB0 + (j + 1) * LANES]
        for j in range(SSM_TILES):
            u_perm = jnp.concatenate(
                [uslab_ref[sb * SSM_TILES + j, pl.ds(s, SCAN_CHUNKS, stride=SCAN_LEN), :]
                 for s in range(SCAN_LEN)], axis=0)
            hs_ref[sb * SUB:(sb + 1) * SUB, j * SSM_TILE_W:(j + 1) * SSM_TILE_W] = _dot(
                u_perm.astype(BF16), bb_ref[j])

    for c0 in list(range(Q0, ZA0, PROJ_CT)) + list(range(UC0, ZC0, PROJ_CT)):
        _proj_chunk(xb, w_in_ref, proj_ref, c0)
    proj_ref[:, VC0:VC0 + BRANCH_W] = _sgu_layernorm(proj_ref[:, VC0:VC0 + BRANCH_W], ptab_ref)

    srow = lax.broadcasted_iota(jnp.int32, (SUBLANES, SSM_HALF), 0)
    for sb, j in [(sb, j) for sb in range(N_SUB) for j in range(SSM_TILES)]:
        r_base = sb * SUB
        re, im = _re(j), _im(j)
        a_r = atab_ref[0, :, re]
        a_i = atab_ref[0, :, im]
        project_late(1)
        if sb == 0:
            c_r = jnp.broadcast_to(hcarry_ref[SUBLANES - 1:SUBLANES, re], (SUBLANES, SSM_HALF))
            c_i = jnp.broadcast_to(hcarry_ref[SUBLANES - 1:SUBLANES, im], (SUBLANES, SSM_HALF))
            h_r = hs_ref[r_base:r_base + SUBLANES, re] + jnp.where(srow == 0, a_r * c_r - a_i * c_i, 0.0)
            h_i = hs_ref[r_base:r_base + SUBLANES, im] + jnp.where(srow == 0, a_r * c_i + a_i * c_r, 0.0)
            hs_ref[r_base:r_base + SUBLANES, re] = h_r
            hs_ref[r_base:r_base + SUBLANES, im] = h_i
        else:
            h_r = hs_ref[r_base:r_base + SUBLANES, re]
            h_i = hs_ref[r_base:r_base + SUBLANES, im]
        for s in range(1, SCAN_LEN):
            rows = slice(r_base + s * SUBLANES, r_base + (s + 1) * SUBLANES)
            h_r, h_i = (hs_ref[rows, re] + (a_r * h_r - a_i * h_i),
                        hs_ref[rows, im] + (a_r * h_i + a_i * h_r))
            hs_ref[rows, re] = h_r
            hs_ref[rows, im] = h_i
        e_r, e_i = h_r, h_i
        for k, d in enumerate((1, 2, 4)):
            if k == 0:
                p_r, p_i = atab_ref[SCAN_LEN - 1, :, re], atab_ref[SCAN_LEN - 1, :, im]
            else:
                p_r, p_i = apw_ref[k - 1, :, re], apw_ref[k - 1, :, im]
            s_r = pltpu.roll(e_r, d, 0)
            s_i = pltpu.roll(e_i, d, 0)
            e_r, e_i = (e_r + jnp.where(srow >= d, p_r * s_r - p_i * s_i, 0.0),
                        e_i + jnp.where(srow >= d, p_r * s_i + p_i * s_r, 0.0))
        if sb == 0:
            first_r = first_i = 0.0
        else:
            first_r = jnp.broadcast_to(hcarry_ref[SUBLANES - 1:SUBLANES, re], (SUBLANES, SSM_HALF))
            first_i = jnp.broadcast_to(hcarry_ref[SUBLANES - 1:SUBLANES, im], (SUBLANES, SSM_HALF))
            q_r, q_i = pchunk_ref[:, re], pchunk_ref[:, im]
            e_r, e_i = (e_r + (q_r * first_r - q_i * first_i),
                        e_i + (q_r * first_i + q_i * first_r))
        hcarry_ref[:, re] = e_r
        hcarry_ref[:, im] = e_i
        hre_ref[0, :, j * SSM_HALF:(j + 1) * SSM_HALF] = e_r
        him_ref[0, :, j * SSM_HALF:(j + 1) * SSM_HALF] = e_i
        in_r = jnp.where(srow == 0, first_r, pltpu.roll(e_r, 1, 0))
        in_i = jnp.where(srow == 0, first_i, pltpu.roll(e_i, 1, 0))
        in_r2 = jnp.concatenate([in_r, in_r], axis=0)
        in_i2 = jnp.concatenate([in_i, in_i], axis=0)
        for s in range(0, SCAN_LEN, 2):
            rows = slice(r_base + s * SUBLANES, r_base + (s + 2) * SUBLANES)
            t_r = jnp.concatenate([atab_ref[s, :, re], atab_ref[s + 1, :, re]], axis=0)
            t_i = jnp.concatenate([atab_ref[s, :, im], atab_ref[s + 1, :, im]], axis=0)
            hsb_ref[rows, re] = (hs_ref[rows, re] + (t_r * in_r2 - t_i * in_i2)).astype(BF16)
            hsb_ref[rows, im] = (hs_ref[rows, im] + (t_r * in_i2 + t_i * in_r2)).astype(BF16)
    assert not late_chunks
    for sb in range(N_SUB):
        for j in range(SSM_TILES):
            y_perm = _dot(hsb_ref[sb * SUB:(sb + 1) * SUB, j * SSM_TILE_W:(j + 1) * SSM_TILE_W],
                          cc_ref[j])
            for s in range(SCAN_LEN):
                yslab_ref[sb * SSM_TILES + j, pl.ds(s, SCAN_CHUNKS, stride=SCAN_LEN), :] = (
                    y_perm[s * SUBLANES:(s + 1) * SUBLANES, :])
    y_lin = jnp.concatenate(
        [jnp.concatenate([yslab_ref[sb * SSM_TILES + j] for j in range(SSM_TILES)], axis=1)
         for sb in range(N_SUB)], axis=0)
    ybr_ref[:, BRANCH_W:2 * BRANCH_W] = _s5_output(y_lin, proj_ref[:, UB0:UB0 + BRANCH_W], ptab_ref, gluw_ref)

    lane2 = lax.broadcasted_iota(jnp.int32, (2 * SUB, LANES), 1)
    kcol = lax.broadcasted_iota(jnp.int32, (SUB, 2 * SUB), 1)
    lane = lax.broadcasted_iota(jnp.int32, (SUB, LANES), 1)
    lo = lane < HEAD_DIM
    first_neg = jnp.where(ji == 0, NEG_INF, 0.0).astype(F32)
    kv_prev = kvprev_ref[...]
    gate_chunks = list(range(0, N_BRANCH * D_MODEL, PROJ_CT))
    per_tile = -(-len(gate_chunks) // (N_SUB * N_HEADS // 2))

    def project_gates(n):
        for _ in range(n):
            if gate_chunks:
                g0 = gate_chunks.pop(0)
                gates_ref[:, g0:g0 + PROJ_CT] = _gate_chunk(xb, w_in_ref, G0 + g0)

    for sb in range(N_SUB):
        rows = slice(sb * SUB, (sb + 1) * SUB)
        kv_cur = proj_ref[rows, K0:K0 + 2 * LANES]
        k_dup = _dup_halves(jnp.concatenate([kv_prev[:, 0:LANES], kv_cur[:, 0:LANES]], axis=0), lane2 < HEAD_DIM)
        v_dup = _dup_halves(jnp.concatenate([kv_prev[:, LANES:], kv_cur[:, LANES:]], axis=0), lane2 < HEAD_DIM)
        k_dup = [k.astype(BF16) for k in k_dup]
        v_dup = [v.astype(BF16) for v in v_dup]
        for t in range(N_HEADS // 2):
            project_gates(per_tile)
            qt = proj_ref[rows, Q0 + t * LANES:Q0 + (t + 1) * LANES]
            kv_head = (2 * t) // GQA_GROUP
            halves = []
            for half in range(2):
                h = 2 * t + half
                qm = jnp.where(lo if half == 0 else jnp.logical_not(lo), qt, 0.0).astype(BF16)
                s = _dot_nt(qm, k_dup[kv_head]) + bias_ref[h]
                if sb == 0:
                    s = s + jnp.where(kcol < SUB, first_neg, 0.0)
                sink = sinks_ref[layer, h]
                m = jnp.maximum(jnp.max(s, axis=-1, keepdims=True), sink)
                p = jnp.exp(s - m)
                denom = jnp.sum(p, axis=-1, keepdims=True) + jnp.exp(sink - m)
                o = _dot(p.astype(BF16), v_dup[kv_head])
                halves.append(o * (1.0 / denom))
            ybr_ref[rows, t * LANES:(t + 1) * LANES] = jnp.where(lo, halves[0], halves[1])
        kv_prev = kv_cur
    kvprev_ref[...] = kv_prev
    kwin_ref[0] = kv_prev[:, 0:LANES]
    vwin_ref[0] = kv_prev[:, LANES:]
    project_gates(len(gate_chunks))

    trow = lax.broadcasted_iota(jnp.int32, (CHUNK, CHUNK), 0)
    tcol = lax.broadcasted_iota(jnp.int32, (CHUNK, CHUNK), 1)
    tril = trow >= tcol
    for g in range(SGU_GROUPS):
        wm = jnp.where(tril, sguw_ref[g], 0.0).astype(BF16)
        b_row = ptab_ref[ROW_SGU_B:ROW_SGU_B + 1, g * CHUNK:(g + 1) * CHUNK]
        b_col = jnp.sum(jnp.where(trow == tcol, b_row, 0.0), axis=1, keepdims=True)
        for sb in range(N_SUB):
            rows = slice(sb * SUB, (sb + 1) * SUB)
            vn_g = proj_ref[rows, VC0 + g * LANES:VC0 + (g + 1) * LANES]
            sg = _dot(wm, vn_g.astype(BF16)) + b_col
            ybr_ref[rows, 2 * BRANCH_W + g * LANES:2 * BRANCH_W + (g + 1) * LANES] = (
                proj_ref[rows, UC0 + g * LANES:UC0 + (g + 1) * LANES] * sg)

    y_ref[0] = _merge(x_ref[0], w_in_ref, proj_ref, ybr_ref, merged_ref, wread_ref, wo_ref, ptab_ref, gates_ref)


def _layer_spec(layer, shape):
    nd = len(shape)
    return pl.BlockSpec((None,) + tuple(shape), lambda *_, _nd=nd: (layer,) + (0,) * _nd,
                        pipeline_mode=pl.Buffered(1))


def _prompt_layer(layer, x, sinks, lw):
    nb, t, _ = x.shape
    grid = (nb, t // TB)
    weights = (lw['w_in'], lw['a_row'], lw['bb'], lw['cc'], lw['ptab'], lw['glu_w'], lw['sgu_w'],
               lw['w_read'], lw['w_o'])
    in_specs = [pl.BlockSpec(memory_space=pltpu.SMEM),
                pl.BlockSpec((1, TB, D_MODEL), lambda b, j: (b, j, 0))]
    in_specs += [_layer_spec(layer, w.shape[1:]) for w in weights]
    out_shape = (jax.ShapeDtypeStruct((nb, t, D_MODEL), F32),
                 jax.ShapeDtypeStruct((nb, SUB, LANES), F32),
                 jax.ShapeDtypeStruct((nb, SUB, LANES), F32),
                 jax.ShapeDtypeStruct((nb, SUBLANES, SSM_W), F32),
                 jax.ShapeDtypeStruct((nb, SUBLANES, SSM_W), F32))
    out_specs = (pl.BlockSpec((1, TB, D_MODEL), lambda b, j: (b, j, 0)),
                 pl.BlockSpec((1, SUB, LANES), lambda b, j: (b, 0, 0)),
                 pl.BlockSpec((1, SUB, LANES), lambda b, j: (b, 0, 0)),
                 pl.BlockSpec((1, SUBLANES, SSM_W), lambda b, j: (b, 0, 0)),
                 pl.BlockSpec((1, SUBLANES, SSM_W), lambda b, j: (b, 0, 0)))
    scratch = [pltpu.VMEM((TB, G0), F32),
               pltpu.VMEM((TB, N_BRANCH * BRANCH_W), F32),
               pltpu.VMEM((SUB, 2 * LANES), F32),
               pltpu.VMEM((N_HEADS, SUB, 2 * SUB), F32),
               pltpu.VMEM((SCAN_LEN, SUBLANES, 2 * SSM_W), F32),
               pltpu.VMEM((2, SUBLANES, 2 * SSM_W), F32),
               pltpu.VMEM((SCAN_CHUNKS, 2 * SSM_W), F32),
               pltpu.VMEM((TB, 2 * SSM_W), F32),
               pltpu.VMEM((SUBLANES, 2 * SSM_W), F32),
               pltpu.VMEM((N_SUB * SSM_TILES, SUB, LANES), F32),
               pltpu.VMEM((N_SUB * SSM_TILES, SUB, LANES), F32),
               pltpu.VMEM((TB, D_MODEL), BF16),
               pltpu.VMEM((TB, N_BRANCH * D_MODEL), F32),
               pltpu.VMEM((TB, 2 * SSM_W), BF16)]
    return pl.pallas_call(
        functools.partial(_prompt_kernel, layer),
        grid=grid,
        in_specs=in_specs,
        out_specs=out_specs,
        out_shape=out_shape,
        scratch_shapes=scratch,
        compiler_params=pltpu.CompilerParams(
            dimension_semantics=("arbitrary", "arbitrary"),
            vmem_limit_bytes=VMEM_LIMIT_BYTES),
        name="prompt_layer",
    )(sinks, x, *weights)


def _sample_kernel(layer, sinks_ref, x_ref, ck_ref, cv_ref, h0re_ref, h0im_ref, w_in_ref, arow_ref, bb_ref, cc_ref,
                   ptab_ref, gluw_ref, sguw_ref, wread_ref, wo_ref,
                   y_ref, kout_ref, vout_ref, hre_ref, him_ref, vn_ref,
                   proj_ref, ybr_ref, merged_ref):
    n_seq = x_ref.shape[0]
    win = ck_ref.shape[1]
    xb = x_ref[...].astype(BF16)
    for c0 in range(0, G0, PROJ_CT):
        _proj_chunk(xb, w_in_ref, proj_ref, c0)
    kout_ref[...] = proj_ref[:, K0:K0 + LANES]
    vout_ref[...] = proj_ref[:, V0:V0 + LANES]

    row = lax.broadcasted_iota(jnp.int32, (SUBLANES, LANES), 0)
    lane = lax.broadcasted_iota(jnp.int32, (SUBLANES, LANES), 1)
    lane_lo = lane < HEAD_DIM
    row_even = (row % 2) == 0
    sel = jnp.logical_or(jnp.logical_and(row_even, lane_lo),
                         jnp.logical_and(jnp.logical_not(row_even), jnp.logical_not(lane_lo)))
    swap = (row // GQA_GROUP) != (row % 2)
    slope = jnp.exp2(-(row.astype(F32) + 1.0))
    bias = -slope * (win - lane).astype(F32)
    sink = jnp.zeros((SUBLANES, 1), F32)
    for h in range(N_HEADS):
        sink = jnp.where(row[:, 0:1] == h, sinks_ref[layer, h], sink)

    def seq_group_step(gi, carry):
        r0 = pl.multiple_of(gi * SUBLANES, SUBLANES)
        q_rows = proj_ref[pl.ds(r0, SUBLANES), Q0:Q0 + BRANCH_W]
        k_rows = proj_ref[pl.ds(r0, SUBLANES), K0:K0 + LANES]
        v_rows = proj_ref[pl.ds(r0, SUBLANES), V0:V0 + LANES]
        q8s, scores = [], []
        for r in range(SUBLANES):
            q8 = jnp.zeros((SUBLANES, LANES), F32)
            for t in range(N_HEADS // 2):
                qt = jnp.broadcast_to(q_rows[r:r + 1, t * LANES:(t + 1) * LANES], (SUBLANES, LANES))
                q8 = jnp.where(jnp.logical_and(sel, row // 2 == t), qt, q8)
            q8 = jnp.where(swap, pltpu.roll(q8, HEAD_DIM, 1), q8)
            q8s.append(q8)
            scores.append(_dot_nt(q8.astype(BF16), ck_ref[r0 + r].astype(BF16)) + bias)
        s = jnp.concatenate(scores, axis=0)
        k_own = jnp.concatenate([jnp.broadcast_to(k_rows[r:r + 1, :], (SUBLANES, LANES))
                                 for r in range(SUBLANES)], axis=0)
        s_own = jnp.sum(jnp.concatenate(q8s, axis=0) * k_own, axis=-1, keepdims=True)
        sink_g = jnp.concatenate([sink] * SUBLANES, axis=0)
        m = jnp.maximum(jnp.maximum(jnp.max(s, axis=-1, keepdims=True), s_own), sink_g)
        p = jnp.exp(s - m)
        p_own = jnp.exp(s_own - m)
        inv = 1.0 / (jnp.sum(p, axis=-1, keepdims=True) + p_own + jnp.exp(sink_g - m))
        outs = [[] for _ in range(N_HEADS // 2)]
        for r in range(SUBLANES):
            rows8 = slice(r * SUBLANES, (r + 1) * SUBLANES)
            o = (_dot(p[rows8, :].astype(BF16), cv_ref[r0 + r].astype(BF16))
                 + p_own[rows8, :] * v_rows[r:r + 1, :]) * inv[rows8, :]
            o = jnp.where(swap, pltpu.roll(o, HEAD_DIM, 1), o)
            o = jnp.where(sel, o, 0.0)
            for t in range(N_HEADS // 2):
                outs[t].append(o[2 * t:2 * t + 1, :] + o[2 * t + 1:2 * t + 2, :])
        for t in range(N_HEADS // 2):
            ybr_ref[pl.ds(r0, SUBLANES), t * LANES:(t + 1) * LANES] = jnp.concatenate(outs[t], axis=0)
        return carry

    lax.fori_loop(0, n_seq // SUBLANES, seq_group_step, 0, unroll=2)

    vn = _sgu_layernorm(proj_ref[:, VC0:VC0 + BRANCH_W], ptab_ref)
    vn_ref[...] = vn
    for g in range(SGU_GROUPS):
        w00 = sguw_ref[g, 0:1, 0:1]
        b0 = ptab_ref[ROW_SGU_B:ROW_SGU_B + 1, g * CHUNK:g * CHUNK + 1]
        cols = slice(g * LANES, (g + 1) * LANES)
        ybr_ref[:, 2 * BRANCH_W + g * LANES:2 * BRANCH_W + (g + 1) * LANES] = (
            proj_ref[:, UC0 + g * LANES:UC0 + (g + 1) * LANES] * (w00 * vn[:, cols] + b0))

    u = proj_ref[:, UB0:UB0 + BRANCH_W]
    y_tiles = []
    for j in range(SSM_TILES):
        re, im = _re(j), _im(j)
        xs = _dot(u[:, j * LANES:(j + 1) * LANES].astype(BF16), bb_ref[j])
        a_r = arow_ref[:, re]
        a_i = arow_ref[:, im]
        natural = slice(j * SSM_HALF, (j + 1) * SSM_HALF)
        h0r = h0re_ref[:, natural]
        h0i = h0im_ref[:, natural]
        hr = xs[:, 0:SSM_HALF] + (a_r * h0r - a_i * h0i)
        hi = xs[:, SSM_HALF:] + (a_r * h0i + a_i * h0r)
        hre_ref[:, natural] = hr
        him_ref[:, natural] = hi
        y_tiles.append(_dot(jnp.concatenate([hr, hi], axis=1).astype(BF16), cc_ref[j]))
    y_lin = jnp.concatenate(y_tiles, axis=1)
    ybr_ref[:, BRANCH_W:2 * BRANCH_W] = _s5_output(y_lin, u, ptab_ref, gluw_ref)

    y_ref[...] = _merge(x_ref[...], w_in_ref, proj_ref, ybr_ref, merged_ref, wread_ref, wo_ref, ptab_ref)


def _sample_layer(layer, x, sinks, ck, cv, h0_re, h0_im, lw):
    n = x.shape[0]
    per_layer = (ck, cv, h0_re, h0_im, lw['w_in'], lw['a_row'], lw['bb'], lw['cc'], lw['ptab'], lw['glu_w'],
                 lw['sgu_w'], lw['w_read'], lw['w_o'])
    in_specs = [pl.BlockSpec(memory_space=pltpu.SMEM),
                pl.BlockSpec(x.shape, lambda i: (0, 0), pipeline_mode=pl.Buffered(1))]
    in_specs += [_layer_spec(layer, a.shape[1:]) for a in per_layer]
    out_shape = (jax.ShapeDtypeStruct((n, D_MODEL), F32),
                 jax.ShapeDtypeStruct((n, LANES), F32),
                 jax.ShapeDtypeStruct((n, LANES), F32),
                 jax.ShapeDtypeStruct((n, SSM_W), F32),
                 jax.ShapeDtypeStruct((n, SSM_W), F32),
                 jax.ShapeDtypeStruct((n, BRANCH_W), F32))
    out_specs = tuple(pl.BlockSpec(s.shape, lambda i: (0, 0)) for s in out_shape)
    scratch = [pltpu.VMEM((n, G0), F32), pltpu.VMEM((n, N_BRANCH * BRANCH_W), F32),
               pltpu.VMEM((n, D_MODEL), BF16)]
    return pl.pallas_call(
        functools.partial(_sample_kernel, layer),
        grid=(1,),
        in_specs=in_specs,
        out_specs=out_specs,
        out_shape=out_shape,
        scratch_shapes=scratch,
        compiler_params=pltpu.CompilerParams(
            dimension_semantics=("arbitrary",), vmem_limit_bytes=VMEM_LIMIT_BYTES),
        name="sample_layer",
    )(sinks, x, *per_layer)


def _param_table(ssm_d, glu_b, sgu_ln_g, sgu_ln_b, ln_g, ln_b, sgu_b):
    depth = ln_g.shape[0]
    rows = [jnp.concatenate([ssm_d, glu_b], axis=1),
            jnp.concatenate([sgu_ln_g, sgu_ln_b], axis=1),
            ln_g, ln_b,
            jnp.concatenate([sgu_b.reshape(depth, BRANCH_W), jnp.zeros((depth, BRANCH_W), F32)], axis=1)]
    rows += [jnp.zeros((depth, D_MODEL), F32)] * (PARAM_ROWS - len(rows))
    return jnp.stack(rows, axis=1)


def kernel(x_prompt, x_sample, cache_k_win, cache_v_win, state_ssm_re, state_ssm_im, w_in, attn_sinks,
           ssm_lambda_re, ssm_lambda_im, ssm_log_dt, ssm_b_re, ssm_b_im, ssm_c_re, ssm_c_im, ssm_d,
           glu_w, glu_b, sgu_ln_g, sgu_ln_b, sgu_w, sgu_b, w_read, w_o, ln_g, ln_b):
    nb, t, _ = x_prompt.shape
    ns = x_sample.shape[0]
    depth, _, win = cache_k_win.shape[:3]
    assert t % TB == 0 and win == WINDOW == SUB and ns % SUBLANES == 0

    a_row, bb, cc = _ssm_prep(ssm_lambda_re, ssm_lambda_im, ssm_log_dt, ssm_b_re, ssm_b_im, ssm_c_re, ssm_c_im)
    lw = dict(w_in=w_in.astype(BF16), a_row=a_row, bb=bb, cc=cc,
              ptab=_param_table(ssm_d, glu_b, sgu_ln_g, sgu_ln_b, ln_g, ln_b, sgu_b),
              glu_w=glu_w.astype(BF16), sgu_w=sgu_w, w_read=w_read.astype(BF16), w_o=w_o.astype(BF16))
    ck = cache_k_win.reshape(depth, ns, win, KV_HEADS * HEAD_DIM)
    cv = cache_v_win.reshape(depth, ns, win, KV_HEADS * HEAD_DIM)
    h0_re = state_ssm_re.reshape(depth, ns, SSM_W)
    h0_im = state_ssm_im.reshape(depth, ns, SSM_W)

    xp = x_prompt
    xs = x_sample.reshape(ns, D_MODEL)
    kp, vp, hrp, hip = [], [], [], []
    ksm, vsm, hrs, his, vcs = [], [], [], [], []
    for l in range(depth):
        xp, kwin, vwin, hre, him = _prompt_layer(l, xp, attn_sinks, lw)
        kp.append(kwin.reshape(nb, SUB, KV_HEADS, HEAD_DIM))
        vp.append(vwin.reshape(nb, SUB, KV_HEADS, HEAD_DIM))
        hrp.append(hre[:, SUBLANES - 1, :].reshape(nb, SSM_GROUPS, SSM_STATE))
        hip.append(him[:, SUBLANES - 1, :].reshape(nb, SSM_GROUPS, SSM_STATE))

        xs, k_s, v_s, hre_s, him_s, vn_s = _sample_layer(l, xs, attn_sinks, ck, cv, h0_re, h0_im, lw)
        ksm.append(k_s.reshape(ns, 1, KV_HEADS, HEAD_DIM))
        vsm.append(v_s.reshape(ns, 1, KV_HEADS, HEAD_DIM))
        hrs.append(hre_s.reshape(ns, SSM_GROUPS, SSM_STATE))
        his.append(him_s.reshape(ns, SSM_GROUPS, SSM_STATE))
        vcs.append(vn_s.reshape(ns, 1, BRANCH_W))
    return (xp, xs.reshape(ns, 1, D_MODEL), jnp.stack(kp), jnp.stack(vp), jnp.stack(hrp), jnp.stack(hip),
            jnp.stack(ksm), jnp.stack(vsm), jnp.stack(hrs), jnp.stack(his), jnp.stack(vcs))
```

```python
import functools
import math

import jax
import jax.numpy as jnp
from jax import lax
from jax.experimental import pallas as pl
from jax.experimental.pallas import tpu as pltpu

D_MODEL = 1024
BRANCH_W = 512
HEAD_DIM = 64
N_HEADS = 8
KV_HEADS = 2
GQA_GROUP = N_HEADS // KV_HEADS
WINDOW = 128
SSM_GROUP_CH = 16
SSM_GROUPS = 32
SSM_STATE = 64
SSM_W = SSM_GROUPS * SSM_STATE
CHUNK = 128
SGU_GROUPS = 4
N_BRANCH = 3
DEPTH = 2
ALPHA = (2 * DEPTH) ** 0.25
LN_EPS = 1e-5
ATTN_SCALE = HEAD_DIM ** -0.5
NEG_INF = -1e30

Q0, K0, V0, ZA0, UB0, ZB0, UC0, VC0, ZC0, G0 = 0, 512, 640, 768, 1280, 1792, 2304, 2816, 3328, 3840
N_IN = 6912

LANES = 128
SUBLANES = 8
MXU_COLS = 256
VMEM_LIMIT_BYTES = 60 * 1024 * 1024

SUB = 128
N_SUB = 2
TB = N_SUB * SUB
PROJ_CT = MXU_COLS

SSM_TILES = BRANCH_W // LANES
SSM_TILE_GROUPS = SSM_GROUPS // SSM_TILES
SSM_HALF = SSM_W // SSM_TILES
SSM_TILE_W = 2 * SSM_HALF
SCAN_CHUNKS = SUBLANES
SCAN_LEN = SUB // SCAN_CHUNKS

ROW_DSKIP_GLUB, ROW_SGU_LN, ROW_LN_G, ROW_LN_B, ROW_SGU_B = 0, 1, 2, 3, 4
PARAM_ROWS = SUBLANES

F32 = jnp.float32
BF16 = jnp.bfloat16


def _re(j):
    return slice(j * SSM_TILE_W, j * SSM_TILE_W + SSM_HALF)


def _im(j):
    return slice(j * SSM_TILE_W + SSM_HALF, (j + 1) * SSM_TILE_W)


def _sigmoid(x):
    return 1.0 / (1.0 + jnp.exp(-x))


def _silu(x):
    return x * _sigmoid(x)


def _gelu_tanh(x):
    c = math.sqrt(2.0 / math.pi)
    return 0.5 * x * (1.0 + jnp.tanh(c * (x + 0.044715 * (x * x * x))))


def _layernorm(x, g, b):
    mu = jnp.mean(x, axis=-1, keepdims=True)
    xc = x - mu
    var = jnp.mean(xc * xc, axis=-1, keepdims=True)
    return xc * lax.rsqrt(var + LN_EPS) * g + b


def _dot(a, b):
    return jnp.dot(a, b, preferred_element_type=F32)


def _dot_nt(a, b):
    return lax.dot_general(a, b, (((1,), (1,)), ((), ())), preferred_element_type=F32)


def _slope(h):
    return 2.0 ** (-(h + 1.0))


def _dup_halves(x, lo):
    sw = pltpu.roll(x, HEAD_DIM, 1)
    return jnp.where(lo, x, sw), jnp.where(lo, sw, x)


def _ssm_prep_kernel(lr_ref, li_ref, ldt_ref, bre_ref, bim_ref, cre_ref, cim_ref,
                     arow_ref, bb_ref, cc_ref, bbs_ref, ccs_ref):
    lr = lr_ref[...]
    li = li_ref[...]
    dt = jnp.exp(ldt_ref[...])
    mag = jnp.exp(lr * dt)
    ar = mag * jnp.cos(li * dt)
    ai = mag * jnp.sin(li * dt)
    den = lr * lr + li * li
    cr = ((ar - 1.0) * lr + ai * li) / den
    ci = (ai * lr - (ar - 1.0) * li) / den
    for j in range(SSM_TILES):
        bbs_ref[...] = jnp.zeros((LANES, SSM_TILE_W), F32)
        ccs_ref[...] = jnp.zeros((SSM_TILE_W, LANES), F32)
        for gl in range(SSM_TILE_GROUPS):
            g = j * SSM_TILE_GROUPS + gl
            cols = slice(gl * SSM_STATE, (gl + 1) * SSM_STATE)
            cols_im = slice(SSM_HALF + gl * SSM_STATE, SSM_HALF + (gl + 1) * SSM_STATE)
            ch = slice(gl * SSM_GROUP_CH, (gl + 1) * SSM_GROUP_CH)
            arow_ref[:, j * SSM_TILE_W + gl * SSM_STATE:j * SSM_TILE_W + (gl + 1) * SSM_STATE] = ar[g:g + 1, :]
            arow_ref[:, j * SSM_TILE_W + SSM_HALF + gl * SSM_STATE:
                     j * SSM_TILE_W + SSM_HALF + (gl + 1) * SSM_STATE] = ai[g:g + 1, :]
            br = bre_ref[g].T
            bi = bim_ref[g].T
            crg = cr[g:g + 1, :]
            cig = ci[g:g + 1, :]
            bbs_ref[ch, cols] = crg * br - cig * bi
            bbs_ref[ch, cols_im] = crg * bi + cig * br
            ccs_ref[cols, ch] = cre_ref[g].T
            ccs_ref[cols_im, ch] = -cim_ref[g].T
        bb_ref[j] = bbs_ref[...].astype(BF16)
        cc_ref[j] = ccs_ref[...].astype(BF16)


def _ssm_prep(lam_re, lam_im, log_dt, b_re, b_im, c_re, c_im):
    depth = lam_re.shape[0]

    def spec(*shape):
        nd = len(shape)
        return pl.BlockSpec((None,) + shape, lambda l, _nd=nd: (l,) + (0,) * _nd)

    return pl.pallas_call(
        _ssm_prep_kernel,
        grid=(depth,),
        in_specs=[spec(SSM_GROUPS, SSM_STATE), spec(SSM_GROUPS, SSM_STATE), spec(SSM_GROUPS, 1),
                  spec(SSM_GROUPS, SSM_STATE, SSM_GROUP_CH), spec(SSM_GROUPS, SSM_STATE, SSM_GROUP_CH),
                  spec(SSM_GROUPS, SSM_GROUP_CH, SSM_STATE), spec(SSM_GROUPS, SSM_GROUP_CH, SSM_STATE)],
        out_specs=(spec(1, 2 * SSM_W), spec(SSM_TILES, LANES, SSM_TILE_W), spec(SSM_TILES, SSM_TILE_W, LANES)),
        out_shape=(jax.ShapeDtypeStruct((depth, 1, 2 * SSM_W), F32),
                   jax.ShapeDtypeStruct((depth, SSM_TILES, LANES, SSM_TILE_W), BF16),
                   jax.ShapeDtypeStruct((depth, SSM_TILES, SSM_TILE_W, LANES), BF16)),
        scratch_shapes=[pltpu.VMEM((LANES, SSM_TILE_W), F32), pltpu.VMEM((SSM_TILE_W, LANES), F32)],
        compiler_params=pltpu.CompilerParams(dimension_semantics=("arbitrary",)),
        name="ssm_prep",
    )(lam_re, lam_im, log_dt.reshape(depth, SSM_GROUPS, 1), b_re, b_im, c_re, c_im)


def _proj_chunk(xb, w_in_ref, proj_ref, c0):
    assert c0 + PROJ_CT <= G0
    y = _dot(xb, w_in_ref[:, c0:c0 + PROJ_CT])
    if any(z0 <= c0 < z0 + BRANCH_W for z0 in (ZA0, ZB0, ZC0)):
        y = _silu(y)
    elif c0 < K0:
        y = y * ATTN_SCALE
    proj_ref[:, c0:c0 + PROJ_CT] = y


def _s5_output(y_lin, u, ptab_ref, gluw_ref):
    y = y_lin + ptab_ref[ROW_DSKIP_GLUB:ROW_DSKIP_GLUB + 1, 0:BRANCH_W] * u
    y = _gelu_tanh(y)
    glu_b = ptab_ref[ROW_DSKIP_GLUB:ROW_DSKIP_GLUB + 1, BRANCH_W:2 * BRANCH_W]
    return y * _sigmoid(_dot(y.astype(BF16), gluw_ref[...]) + glu_b)


def _gate_chunk(xb, w_in_ref, g0):
    return _sigmoid(_dot(xb, w_in_ref[:, g0:g0 + PROJ_CT]))


def _merge(x, w_in_ref, proj_ref, ybr_ref, merged_ref, wread_ref, wo_ref, ptab_ref, gates_ref=None):
    xb = x.astype(BF16)
    ybs = [(ybr_ref[:, b * BRANCH_W:(b + 1) * BRANCH_W] * proj_ref[:, z0:z0 + BRANCH_W]).astype(BF16)
           for b, z0 in enumerate((ZA0, ZB0, ZC0))]
    for n0 in range(0, D_MODEL, PROJ_CT):
        merged = None
        for b in range(N_BRANCH):
            g0 = b * D_MODEL + n0
            gate = (_gate_chunk(xb, w_in_ref, G0 + g0) if gates_ref is None
                    else gates_ref[:, g0:g0 + PROJ_CT])
            term = gate * _dot(ybs[b], wread_ref[b, :, n0:n0 + PROJ_CT])
            merged = term if merged is None else merged + term
        merged_ref[:, n0:n0 + PROJ_CT] = merged.astype(BF16)
    out = _dot(merged_ref[...], wo_ref[...])
    return _layernorm(ALPHA * x + out, ptab_ref[ROW_LN_G:ROW_LN_G + 1, :], ptab_ref[ROW_LN_B:ROW_LN_B + 1, :])


def _sgu_layernorm(v, ptab_ref):
    return _layernorm(v, ptab_ref[ROW_SGU_LN:ROW_SGU_LN + 1, 0:BRANCH_W],
                      ptab_ref[ROW_SGU_LN:ROW_SGU_LN + 1, BRANCH_W:2 * BRANCH_W])


def _prompt_kernel(layer, sinks_ref, x_ref, w_in_ref, arow_ref, bb_ref, cc_ref, ptab_ref, gluw_ref,
                   sguw_ref, wread_ref, wo_ref,
                   y_ref, kwin_ref, vwin_ref, hre_ref, him_ref,
                   proj_ref, ybr_ref, kvprev_ref, bias_ref, atab_ref, apw_ref, hs_ref, hcarry_ref,
                   uslab_ref, yslab_ref, merged_ref, gates_ref, hsb_ref):
    bi = pl.program_id(0)
    ji = pl.program_id(1)

    @pl.when(jnp.logical_and(bi == 0, ji == 0))
    def _init_tables():
        qa = lax.broadcasted_iota(jnp.int32, (SUB, 2 * SUB), 0)
        kk = lax.broadcasted_iota(jnp.int32, (SUB, 2 * SUB), 1)
        dist = qa + WINDOW - kk
        valid = jnp.logical_and(dist >= 0, dist <= WINDOW)
        distf = dist.astype(F32)
        for h in range(N_HEADS):
            bias_ref[h] = jnp.where(valid, -_slope(h) * distf, NEG_INF)
        for j in range(SSM_TILES):
            a_r = arow_ref[:, _re(j)]
            a_i = arow_ref[:, _im(j)]
            pr, pi = a_r, a_i
            for s in range(SCAN_LEN):
                atab_ref[s, :, _re(j)] = jnp.broadcast_to(pr, (SUBLANES, SSM_HALF))
                atab_ref[s, :, _im(j)] = jnp.broadcast_to(pi, (SUBLANES, SSM_HALF))
                if s + 1 < SCAN_LEN:
                    pr, pi = pr * a_r - pi * a_i, pr * a_i + pi * a_r
            for k in range(2):
                pr, pi = pr * pr - pi * pi, 2.0 * (pr * pi)
                apw_ref[k, :, _re(j)] = jnp.broadcast_to(pr, (SUBLANES, SSM_HALF))
                apw_ref[k, :, _im(j)] = jnp.broadcast_to(pi, (SUBLANES, SSM_HALF))

    @pl.when(ji == 0)
    def _reset_carries():
        kvprev_ref[...] = jnp.zeros((SUB, 2 * LANES), F32)
        hcarry_ref[...] = jnp.zeros((SUBLANES, 2 * SSM_W), F32)

    xb = x_ref[0].astype(BF16)
    late_chunks = [z0 + c for z0 in (ZA0, ZB0, ZC0) for c in range(0, BRANCH_W, PROJ_CT)]

    def project_late(n):
        for _ in range(n):
            if late_chunks:
                _proj_chunk(xb, w_in_ref, proj_ref, late_chunks.pop(0))

    for c0 in range(UB0, UB0 + BRANCH_W, PROJ_CT):
        _proj_chunk(xb, w_in_ref, proj_ref, c0)

    for sb in range(N_SUB):
        for j in range(SSM_TILES):
            uslab_ref[sb * SSM_TILES + j] = proj_ref[sb * SUB:(sb + 1) * SUB, UB0 + j * LANES:UB0 + (j + 1) * LANES]
        for j in range(SSM_TILES):
            u_perm = jnp.concatenate(
                [uslab_ref[sb * SSM_TILES + j, pl.ds(s, SCAN_CHUNKS, stride=SCAN_LEN), :]
                 for s in range(SCAN_LEN)], axis=0)
            hs_ref[sb * SUB:(sb + 1) * SUB, j * SSM_TILE_W:(j + 1) * SSM_TILE_W] = _dot(
                u_perm.astype(BF16), bb_ref[j])

    for c0 in list(range(Q0, ZA0, PROJ_CT)) + list(range(UC0, ZC0, PROJ_CT)):
        _proj_chunk(xb, w_in_ref, proj_ref, c0)
    proj_ref[:, VC0:VC0 + BRANCH_W] = _sgu_layernorm(proj_ref[:, VC0:VC0 + BRANCH_W], ptab_ref)

    srow = lax.broadcasted_iota(jnp.int32, (SUBLANES, SSM_HALF), 0)
    for sb, j in [(sb, j) for sb in range(N_SUB) for j in range(SSM_TILES)]:
        r_base = sb * SUB
        re, im = _re(j), _im(j)
        a_r = atab_ref[0, :, re]
        a_i = atab_ref[0, :, im]
        project_late(1)
        c_r = jnp.broadcast_to(hcarry_ref[SUBLANES - 1:SUBLANES, re], (SUBLANES, SSM_HALF))
        c_i = jnp.broadcast_to(hcarry_ref[SUBLANES - 1:SUBLANES, im], (SUBLANES, SSM_HALF))
        h_r = hs_ref[r_base:r_base + SUBLANES, re] + jnp.where(srow == 0, a_r * c_r - a_i * c_i, 0.0)
        h_i = hs_ref[r_base:r_base + SUBLANES, im] + jnp.where(srow == 0, a_r * c_i + a_i * c_r, 0.0)
        hs_ref[r_base:r_base + SUBLANES, re] = h_r
        hs_ref[r_base:r_base + SUBLANES, im] = h_i
        for s in range(1, SCAN_LEN):
            rows = slice(r_base + s * SUBLANES, r_base + (s + 1) * SUBLANES)
            h_r, h_i = (hs_ref[rows, re] + (a_r * h_r - a_i * h_i),
                        hs_ref[rows, im] + (a_r * h_i + a_i * h_r))
            hs_ref[rows, re] = h_r
            hs_ref[rows, im] = h_i
        e_r, e_i = h_r, h_i
        for k, d in enumerate((1, 2, 4)):
            if k == 0:
                p_r, p_i = atab_ref[SCAN_LEN - 1, :, re], atab_ref[SCAN_LEN - 1, :, im]
            else:
                p_r, p_i = apw_ref[k - 1, :, re], apw_ref[k - 1, :, im]
            s_r = pltpu.roll(e_r, d, 0)
            s_i = pltpu.roll(e_i, d, 0)
            e_r, e_i = (e_r + jnp.where(srow >= d, p_r * s_r - p_i * s_i, 0.0),
                        e_i + jnp.where(srow >= d, p_r * s_i + p_i * s_r, 0.0))
        hcarry_ref[:, re] = e_r
        hcarry_ref[:, im] = e_i
        hre_ref[0, :, j * SSM_HALF:(j + 1) * SSM_HALF] = e_r
        him_ref[0, :, j * SSM_HALF:(j + 1) * SSM_HALF] = e_i
        in_r = jnp.where(srow == 0, 0.0, pltpu.roll(e_r, 1, 0))
        in_i = jnp.where(srow == 0, 0.0, pltpu.roll(e_i, 1, 0))
        in_r2 = jnp.concatenate([in_r, in_r], axis=0)
        in_i2 = jnp.concatenate([in_i, in_i], axis=0)
        for s in range(0, SCAN_LEN, 2):
            rows = slice(r_base + s * SUBLANES, r_base + (s + 2) * SUBLANES)
            t_r = jnp.concatenate([atab_ref[s, :, re], atab_ref[s + 1, :, re]], axis=0)
            t_i = jnp.concatenate([atab_ref[s, :, im], atab_ref[s + 1, :, im]], axis=0)
            hsb_ref[rows, re] = (hs_ref[rows, re] + (t_r * in_r2 - t_i * in_i2)).astype(BF16)
            hsb_ref[rows, im] = (hs_ref[rows, im] + (t_r * in_i2 + t_i * in_r2)).astype(BF16)
    assert not late_chunks
    for sb in range(N_SUB):
        for j in range(SSM_TILES):
            y_perm = _dot(hsb_ref[sb * SUB:(sb + 1) * SUB, j * SSM_TILE_W:(j + 1) * SSM_TILE_W],
                          cc_ref[j])
            for s in range(SCAN_LEN):
                yslab_ref[sb * SSM_TILES + j, pl.ds(s, SCAN_CHUNKS, stride=SCAN_LEN), :] = (
                    y_perm[s * SUBLANES:(s + 1) * SUBLANES, :])
    y_lin = jnp.concatenate(
        [jnp.concatenate([yslab_ref[sb * SSM_TILES + j] for j in range(SSM_TILES)], axis=1)
         for sb in range(N_SUB)], axis=0)
    ybr_ref[:, BRANCH_W:2 * BRANCH_W] = _s5_output(y_lin, proj_ref[:, UB0:UB0 + BRANCH_W], ptab_ref, gluw_ref)

    lane2 = lax.broadcasted_iota(jnp.int32, (2 * SUB, LANES), 1)
    kcol = lax.broadcasted_iota(jnp.int32, (SUB, 2 * SUB), 1)
    lane = lax.broadcasted_iota(jnp.int32, (SUB, LANES), 1)
    lo = lane < HEAD_DIM
    first_neg = jnp.where(ji == 0, NEG_INF, 0.0).astype(F32)
    kv_prev = kvprev_ref[...]
    gate_chunks = list(range(0, N_BRANCH * D_MODEL, PROJ_CT))
    per_tile = -(-len(gate_chunks) // (N_SUB * N_HEADS // 2))

    def project_gates(n):
        for _ in range(n):
            if gate_chunks:
                g0 = gate_chunks.pop(0)
                gates_ref[:, g0:g0 + PROJ_CT] = _gate_chunk(xb, w_in_ref, G0 + g0)

    for sb in range(N_SUB):
        rows = slice(sb * SUB, (sb + 1) * SUB)
        kv_cur = proj_ref[rows, K0:K0 + 2 * LANES]
        k_dup = _dup_halves(jnp.concatenate([kv_prev[:, 0:LANES], kv_cur[:, 0:LANES]], axis=0), lane2 < HEAD_DIM)
        v_dup = _dup_halves(jnp.concatenate([kv_prev[:, LANES:], kv_cur[:, LANES:]], axis=0), lane2 < HEAD_DIM)
        k_dup = [k.astype(BF16) for k in k_dup]
        v_dup = [v.astype(BF16) for v in v_dup]
        for t in range(N_HEADS // 2):
            project_gates(per_tile)
            qt = proj_ref[rows, Q0 + t * LANES:Q0 + (t + 1) * LANES]
            kv_head = (2 * t) // GQA_GROUP
            halves = []
            for half in range(2):
                h = 2 * t + half
                qm = jnp.where(lo if half == 0 else jnp.logical_not(lo), qt, 0.0).astype(BF16)
                s = _dot_nt(qm, k_dup[kv_head]) + bias_ref[h]
                if sb == 0:
                    s = s + jnp.where(kcol < SUB, first_neg, 0.0)
                sink = sinks_ref[layer, h]
                m = jnp.maximum(jnp.max(s, axis=-1, keepdims=True), sink)
                p = jnp.exp(s - m)
                denom = jnp.sum(p, axis=-1, keepdims=True) + jnp.exp(sink - m)
                o = _dot(p.astype(BF16), v_dup[kv_head])
                halves.append(o * (1.0 / denom))
            ybr_ref[rows, t * LANES:(t + 1) * LANES] = jnp.where(lo, halves[0], halves[1])
        kv_prev = kv_cur
    kvprev_ref[...] = kv_prev
    kwin_ref[0] = kv_prev[:, 0:LANES]
    vwin_ref[0] = kv_prev[:, LANES:]
    project_gates(len(gate_chunks))

    trow = lax.broadcasted_iota(jnp.int32, (CHUNK, CHUNK), 0)
    tcol = lax.broadcasted_iota(jnp.int32, (CHUNK, CHUNK), 1)
    tril = trow >= tcol
    for g in range(SGU_GROUPS):
        wm = jnp.where(tril, sguw_ref[g], 0.0).astype(BF16)
        b_row = ptab_ref[ROW_SGU_B:ROW_SGU_B + 1, g * CHUNK:(g + 1) * CHUNK]
        b_col = jnp.sum(jnp.where(trow == tcol, b_row, 0.0), axis=1, keepdims=True)
        for sb in range(N_SUB):
            rows = slice(sb * SUB, (sb + 1) * SUB)
            vn_g = proj_ref[rows, VC0 + g * LANES:VC0 + (g + 1) * LANES]
            sg = _dot(wm, vn_g.astype(BF16)) + b_col
            ybr_ref[rows, 2 * BRANCH_W + g * LANES:2 * BRANCH_W + (g + 1) * LANES] = (
                proj_ref[rows, UC0 + g * LANES:UC0 + (g + 1) * LANES] * sg)

    y_ref[0] = _merge(x_ref[0], w_in_ref, proj_ref, ybr_ref, merged_ref, wread_ref, wo_ref, ptab_ref, gates_ref)


def _layer_spec(layer, shape):
    nd = len(shape)
    return pl.BlockSpec((None,) + tuple(shape), lambda *_, _nd=nd: (layer,) + (0,) * _nd,
                        pipeline_mode=pl.Buffered(1))


def _prompt_layer(layer, x, sinks, lw):
    nb, t, _ = x.shape
    grid = (nb, t // TB)
    weights = (lw['w_in'], lw['a_row'], lw['bb'], lw['cc'], lw['ptab'], lw['glu_w'], lw['sgu_w'],
               lw['w_read'], lw['w_o'])
    in_specs = [pl.BlockSpec(memory_space=pltpu.SMEM),
                pl.BlockSpec((1, TB, D_MODEL), lambda b, j: (b, j, 0))]
    in_specs += [_layer_spec(layer, w.shape[1:]) for w in weights]
    out_shape = (jax.ShapeDtypeStruct((nb, t, D_MODEL), F32),
                 jax.ShapeDtypeStruct((nb, SUB, LANES), F32),
                 jax.ShapeDtypeStruct((nb, SUB, LANES), F32),
                 jax.ShapeDtypeStruct((nb, SUBLANES, SSM_W), F32),
                 jax.ShapeDtypeStruct((nb, SUBLANES, SSM_W), F32))
    out_specs = (pl.BlockSpec((1, TB, D_MODEL), lambda b, j: (b, j, 0)),
                 pl.BlockSpec((1, SUB, LANES), lambda b, j: (b, 0, 0)),
                 pl.BlockSpec((1, SUB, LANES), lambda b, j: (b, 0, 0)),
                 pl.BlockSpec((1, SUBLANES, SSM_W), lambda b, j: (b, 0, 0)),
                 pl.BlockSpec((1, SUBLANES, SSM_W), lambda b, j: (b, 0, 0)))
    scratch = [pltpu.VMEM((TB, G0), F32),
               pltpu.VMEM((TB, N_BRANCH * BRANCH_W), F32),
               pltpu.VMEM((SUB, 2 * LANES), F32),
               pltpu.VMEM((N_HEADS, SUB, 2 * SUB), F32),
               pltpu.VMEM((SCAN_LEN, SUBLANES, 2 * SSM_W), F32),
               pltpu.VMEM((2, SUBLANES, 2 * SSM_W), F32),
               pltpu.VMEM((TB, 2 * SSM_W), F32),
               pltpu.VMEM((SUBLANES, 2 * SSM_W), F32),
               pltpu.VMEM((N_SUB * SSM_TILES, SUB, LANES), F32),
               pltpu.VMEM((N_SUB * SSM_TILES, SUB, LANES), F32),
               pltpu.VMEM((TB, D_MODEL), BF16),
               pltpu.VMEM((TB, N_BRANCH * D_MODEL), F32),
               pltpu.VMEM((TB, 2 * SSM_W), BF16)]
    return pl.pallas_call(
        functools.partial(_prompt_kernel, layer),
        grid=grid,
        in_specs=in_specs,
        out_specs=out_specs,
        out_shape=out_shape,
        scratch_shapes=scratch,
        compiler_params=pltpu.CompilerParams(
            dimension_semantics=("arbitrary", "arbitrary"),
            vmem_limit_bytes=VMEM_LIMIT_BYTES),
        name="prompt_layer",
    )(sinks, x, *weights)


def _sample_kernel(layer, sinks_ref, x_ref, ck_ref, cv_ref, h0re_ref, h0im_ref, w_in_ref, arow_ref, bb_ref, cc_ref,
                   ptab_ref, gluw_ref, sguw_ref, wread_ref, wo_ref,
                   y_ref, kout_ref, vout_ref, hre_ref, him_ref, vn_ref,
                   proj_ref, ybr_ref, merged_ref):
    n_seq = x_ref.shape[0]
    win = ck_ref.shape[1]
    xb = x_ref[...].astype(BF16)
    for c0 in range(0, G0, PROJ_CT):
        _proj_chunk(xb, w_in_ref, proj_ref, c0)
    kout_ref[...] = proj_ref[:, K0:K0 + LANES]
    vout_ref[...] = proj_ref[:, V0:V0 + LANES]

    row = lax.broadcasted_iota(jnp.int32, (SUBLANES, LANES), 0)
    lane = lax.broadcasted_iota(jnp.int32, (SUBLANES, LANES), 1)
    lane_lo = lane < HEAD_DIM
    row_even = (row % 2) == 0
    sel = jnp.logical_or(jnp.logical_and(row_even, lane_lo),
                         jnp.logical_and(jnp.logical_not(row_even), jnp.logical_not(lane_lo)))
    swap = (row // GQA_GROUP) != (row % 2)
    slope = jnp.exp2(-(row.astype(F32) + 1.0))
    bias = -slope * (win - lane).astype(F32)
    sink = jnp.zeros((SUBLANES, 1), F32)
    for h in range(N_HEADS):
        sink = jnp.where(row[:, 0:1] == h, sinks_ref[layer, h], sink)

    def seq_group_step(gi, carry):
        r0 = pl.multiple_of(gi * SUBLANES, SUBLANES)
        q_rows = proj_ref[pl.ds(r0, SUBLANES), Q0:Q0 + BRANCH_W]
        k_rows = proj_ref[pl.ds(r0, SUBLANES), K0:K0 + LANES]
        v_rows = proj_ref[pl.ds(r0, SUBLANES), V0:V0 + LANES]
        q8s, scores = [], []
        for r in range(SUBLANES):
            q8 = jnp.zeros((SUBLANES, LANES), F32)
            for t in range(N_HEADS // 2):
                qt = jnp.broadcast_to(q_rows[r:r + 1, t * LANES:(t + 1) * LANES], (SUBLANES, LANES))
                q8 = jnp.where(jnp.logical_and(sel, row // 2 == t), qt, q8)
            q8 = jnp.where(swap, pltpu.roll(q8, HEAD_DIM, 1), q8)
            q8s.append(q8)
            scores.append(_dot_nt(q8.astype(BF16), ck_ref[r0 + r].astype(BF16)) + bias)
        s = jnp.concatenate(scores, axis=0)
        k_own = jnp.concatenate([jnp.broadcast_to(k_rows[r:r + 1, :], (SUBLANES, LANES))
                                 for r in range(SUBLANES)], axis=0)
        s_own = jnp.sum(jnp.concatenate(q8s, axis=0) * k_own, axis=-1, keepdims=True)
        sink_g = jnp.concatenate([sink] * SUBLANES, axis=0)
        m = jnp.maximum(jnp.maximum(jnp.max(s, axis=-1, keepdims=True), s_own), sink_g)
        p = jnp.exp(s - m)
        p_own = jnp.exp(s_own - m)
        inv = 1.0 / (jnp.sum(p, axis=-1, keepdims=True) + p_own + jnp.exp(sink_g - m))
        outs = [[] for _ in range(N_HEADS // 2)]
        for r in range(SUBLANES):
            rows8 = slice(r * SUBLANES, (r + 1) * SUBLANES)
            o = (_dot(p[rows8, :].astype(BF16), cv_ref[r0 + r].astype(BF16))
                 + p_own[rows8, :] * v_rows[r:r + 1, :]) * inv[rows8, :]
            o = jnp.where(swap, pltpu.roll(o, HEAD_DIM, 1), o)
            o = jnp.where(sel, o, 0.0)
            for t in range(N_HEADS // 2):
                outs[t].append(o[2 * t:2 * t + 1, :] + o[2 * t + 1:2 * t + 2, :])
        for t in range(N_HEADS // 2):
            ybr_ref[pl.ds(r0, SUBLANES), t * LANES:(t + 1) * LANES] = jnp.concatenate(outs[t], axis=0)
        return carry

    lax.fori_loop(0, n_seq // SUBLANES, seq_group_step, 0, unroll=2)

    vn = _sgu_layernorm(proj_ref[:, VC0:VC0 + BRANCH_W], ptab_ref)
    vn_ref[...] = vn
    for g in range(SGU_GROUPS):
        w00 = sguw_ref[g, 0:1, 0:1]
        b0 = ptab_ref[ROW_SGU_B:ROW_SGU_B + 1, g * CHUNK:g * CHUNK + 1]
        cols = slice(g * LANES, (g + 1) * LANES)
        ybr_ref[:, 2 * BRANCH_W + g * LANES:2 * BRANCH_W + (g + 1) * LANES] = (
            proj_ref[:, UC0 + g * LANES:UC0 + (g + 1) * LANES] * (w00 * vn[:, cols] + b0))

    u = proj_ref[:, UB0:UB0 + BRANCH_W]
    y_tiles = []
    for j in range(SSM_TILES):
        re, im = _re(j), _im(j)
        xs = _dot(u[:, j * LANES:(j + 1) * LANES].astype(BF16), bb_ref[j])
        a_r = arow_ref[:, re]
        a_i = arow_ref[:, im]
        natural = slice(j * SSM_HALF, (j + 1) * SSM_HALF)
        h0r = h0re_ref[:, natural]
        h0i = h0im_ref[:, natural]
        hr = xs[:, 0:SSM_HALF] + (a_r * h0r - a_i * h0i)
        hi = xs[:, SSM_HALF:] + (a_r * h0i + a_i * h0r)
        hre_ref[:, natural] = hr
        him_ref[:, natural] = hi
        y_tiles.append(_dot(jnp.concatenate([hr, hi], axis=1).astype(BF16), cc_ref[j]))
    y_lin = jnp.concatenate(y_tiles, axis=1)
    ybr_ref[:, BRANCH_W:2 * BRANCH_W] = _s5_output(y_lin, u, ptab_ref, gluw_ref)

    y_ref[...] = _merge(x_ref[...], w_in_ref, proj_ref, ybr_ref, merged_ref, wread_ref, wo_ref, ptab_ref)


def _sample_layer(layer, x, sinks, ck, cv, h0_re, h0_im, lw):
    n = x.shape[0]
    per_layer = (ck, cv, h0_re, h0_im, lw['w_in'], lw['a_row'], lw['bb'], lw['cc'], lw['ptab'], lw['glu_w'],
                 lw['sgu_w'], lw['w_read'], lw['w_o'])
    in_specs = [pl.BlockSpec(memory_space=pltpu.SMEM),
                pl.BlockSpec(x.shape, lambda i: (0, 0), pipeline_mode=pl.Buffered(1))]
    in_specs += [_layer_spec(layer, a.shape[1:]) for a in per_layer]
    out_shape = (jax.ShapeDtypeStruct((n, D_MODEL), F32),
                 jax.ShapeDtypeStruct((n, LANES), F32),
                 jax.ShapeDtypeStruct((n, LANES), F32),
                 jax.ShapeDtypeStruct((n, SSM_W), F32),
                 jax.ShapeDtypeStruct((n, SSM_W), F32),
                 jax.ShapeDtypeStruct((n, BRANCH_W), F32))
    out_specs = tuple(pl.BlockSpec(s.shape, lambda i: (0, 0)) for s in out_shape)
    scratch = [pltpu.VMEM((n, G0), F32), pltpu.VMEM((n, N_BRANCH * BRANCH_W), F32),
               pltpu.VMEM((n, D_MODEL), BF16)]
    return pl.pallas_call(
        functools.partial(_sample_kernel, layer),
        grid=(1,),
        in_specs=in_specs,
        out_specs=out_specs,
        out_shape=out_shape,
        scratch_shapes=scratch,
        compiler_params=pltpu.CompilerParams(
            dimension_semantics=("arbitrary",), vmem_limit_bytes=VMEM_LIMIT_BYTES),
        name="sample_layer",
    )(sinks, x, *per_layer)


def _param_table(ssm_d, glu_b, sgu_ln_g, sgu_ln_b, ln_g, ln_b, sgu_b):
    depth = ln_g.shape[0]
    rows = [jnp.concatenate([ssm_d, glu_b], axis=1),
            jnp.concatenate([sgu_ln_g, sgu_ln_b], axis=1),
            ln_g, ln_b,
            jnp.concatenate([sgu_b.reshape(depth, BRANCH_W), jnp.zeros((depth, BRANCH_W), F32)], axis=1)]
    rows += [jnp.zeros((depth, D_MODEL), F32)] * (PARAM_ROWS - len(rows))
    return jnp.stack(rows, axis=1)


def kernel(x_prompt, x_sample, cache_k_win, cache_v_win, state_ssm_re, state_ssm_im, w_in, attn_sinks,
           ssm_lambda_re, ssm_lambda_im, ssm_log_dt, ssm_b_re, ssm_b_im, ssm_c_re, ssm_c_im, ssm_d,
           glu_w, glu_b, sgu_ln_g, sgu_ln_b, sgu_w, sgu_b, w_read, w_o, ln_g, ln_b):
    nb, t, _ = x_prompt.shape
    ns = x_sample.shape[0]
    depth, _, win = cache_k_win.shape[:3]
    assert t % TB == 0 and win == WINDOW == SUB and ns % SUBLANES == 0

    a_row, bb, cc = _ssm_prep(ssm_lambda_re, ssm_lambda_im, ssm_log_dt, ssm_b_re, ssm_b_im, ssm_c_re, ssm_c_im)
    lw = dict(w_in=w_in.astype(BF16), a_row=a_row, bb=bb, cc=cc,
              ptab=_param_table(ssm_d, glu_b, sgu_ln_g, sgu_ln_b, ln_g, ln_b, sgu_b),
              glu_w=glu_w.astype(BF16), sgu_w=sgu_w, w_read=w_read.astype(BF16), w_o=w_o.astype(BF16))
    ck = cache_k_win.reshape(depth, ns, win, KV_HEADS * HEAD_DIM)
    cv = cache_v_win.reshape(depth, ns, win, KV_HEADS * HEAD_DIM)
    h0_re = state_ssm_re.reshape(depth, ns, SSM_W)
    h0_im = state_ssm_im.reshape(depth, ns, SSM_W)

    xp = x_prompt
    xs = x_sample.reshape(ns, D_MODEL)
    kp, vp, hrp, hip = [], [], [], []
    ksm, vsm, hrs, his, vcs = [], [], [], [], []
    for l in range(depth):
        xp, kwin, vwin, hre, him = _prompt_layer(l, xp, attn_sinks, lw)
        kp.append(kwin.reshape(nb, SUB, KV_HEADS, HEAD_DIM))
        vp.append(vwin.reshape(nb, SUB, KV_HEADS, HEAD_DIM))
        hrp.append(hre[:, SUBLANES - 1, :].reshape(nb, SSM_GROUPS, SSM_STATE))
        hip.append(him[:, SUBLANES - 1, :].reshape(nb, SSM_GROUPS, SSM_STATE))

        xs, k_s, v_s, hre_s, him_s, vn_s = _sample_layer(l, xs, attn_sinks, ck, cv, h0_re, h0_im, lw)
        ksm.append(k_s.reshape(ns, 1, KV_HEADS, HEAD_DIM))
        vsm.append(v_s.reshape(ns, 1, KV_HEADS, HEAD_DIM))
        hrs.append(hre_s.reshape(ns, SSM_GROUPS, SSM_STATE))
        his.append(him_s.reshape(ns, SSM_GROUPS, SSM_STATE))
        vcs.append(vn_s.reshape(ns, 1, BRANCH_W))
    return (xp, xs.reshape(ns, 1, D_MODEL), jnp.stack(kp), jnp.stack(vp), jnp.stack(hrp), jnp.stack(hip),
            jnp.stack(ksm), jnp.stack(vsm), jnp.stack(hrs), jnp.stack(his), jnp.stack(vcs))
```

```python
import functools
import math

import jax
import jax.numpy as jnp
from jax import lax
from jax.experimental import pallas as pl
from jax.experimental.pallas import tpu as pltpu

D_MODEL = 1024
BRANCH_W = 512
HEAD_DIM = 64
N_HEADS = 8
KV_HEADS = 2
GQA_GROUP = N_HEADS // KV_HEADS
WINDOW = 128
SSM_GROUP_CH = 16
SSM_GROUPS = 32
SSM_STATE = 64
SSM_W = SSM_GROUPS * SSM_STATE
CHUNK = 128
SGU_GROUPS = 4
N_BRANCH = 3
DEPTH = 2
ALPHA = (2 * DEPTH) ** 0.25
LN_EPS = 1e-5
ATTN_SCALE = HEAD_DIM ** -0.5
NEG_INF = -1e30

Q0, K0, V0, ZA0, UB0, ZB0, UC0, VC0, ZC0, G0 = 0, 512, 640, 768, 1280, 1792, 2304, 2816, 3328, 3840
N_IN = 6912

LANES = 128
SUBLANES = 8
MXU_COLS = 256
VMEM_LIMIT_BYTES = 60 * 1024 * 1024

SUB = 128
N_SUB = 2
TB = N_SUB * SUB
PROJ_CT = MXU_COLS

SSM_TILES = BRANCH_W // LANES
SSM_TILE_GROUPS = SSM_GROUPS // SSM_TILES
SSM_HALF = SSM_W // SSM_TILES
SSM_TILE_W = 2 * SSM_HALF
SCAN_CHUNKS = SUBLANES
SCAN_LEN = SUB // SCAN_CHUNKS

ROW_DSKIP_GLUB, ROW_SGU_LN, ROW_LN_G, ROW_LN_B, ROW_SGU_B = 0, 1, 2, 3, 4
PARAM_ROWS = SUBLANES

F32 = jnp.float32
BF16 = jnp.bfloat16


def _re(j):
    return slice(j * SSM_TILE_W, j * SSM_TILE_W + SSM_HALF)


def _im(j):
    return slice(j * SSM_TILE_W + SSM_HALF, (j + 1) * SSM_TILE_W)


def _sigmoid(x):
    return 1.0 / (1.0 + jnp.exp(-x))


def _silu(x):
    return x * _sigmoid(x)


def _gelu_tanh(x):
    c = math.sqrt(2.0 / math.pi)
    return 0.5 * x * (1.0 + jnp.tanh(c * (x + 0.044715 * (x * x * x))))


def _layernorm(x, g, b):
    mu = jnp.mean(x, axis=-1, keepdims=True)
    xc = x - mu
    var = jnp.mean(xc * xc, axis=-1, keepdims=True)
    return xc * lax.rsqrt(var + LN_EPS) * g + b


def _dot(a, b):
    return jnp.dot(a, b, preferred_element_type=F32)


def _dot_nt(a, b):
    return lax.dot_general(a, b, (((1,), (1,)), ((), ())), preferred_element_type=F32)


def _slope(h):
    return 2.0 ** (-(h + 1.0))


def _dup_halves(x, lo):
    sw = pltpu.roll(x, HEAD_DIM, 1)
    return jnp.where(lo, x, sw), jnp.where(lo, sw, x)


def _ssm_prep_kernel(lr_ref, li_ref, ldt_ref, brt_ref, bit_ref, crt_ref, cit_ref,
                     arow_ref, bb_ref, cc_ref, bbs_ref, ccs_ref):
    lr = lr_ref[...]
    li = li_ref[...]
    dt = jnp.exp(ldt_ref[...])
    mag = jnp.exp(lr * dt)
    ar = mag * jnp.cos(li * dt)
    ai = mag * jnp.sin(li * dt)
    den = lr * lr + li * li
    cr = ((ar - 1.0) * lr + ai * li) / den
    ci = (ai * lr - (ar - 1.0) * li) / den
    for j in range(SSM_TILES):
        bbs_ref[...] = jnp.zeros((LANES, SSM_TILE_W), F32)
        ccs_ref[...] = jnp.zeros((SSM_TILE_W, LANES), F32)
        for gl in range(SSM_TILE_GROUPS):
            g = j * SSM_TILE_GROUPS + gl
            cols = slice(gl * SSM_STATE, (gl + 1) * SSM_STATE)
            cols_im = slice(SSM_HALF + gl * SSM_STATE, SSM_HALF + (gl + 1) * SSM_STATE)
            ch = slice(gl * SSM_GROUP_CH, (gl + 1) * SSM_GROUP_CH)
            arow_ref[:, j * SSM_TILE_W + gl * SSM_STATE:j * SSM_TILE_W + (gl + 1) * SSM_STATE] = ar[g:g + 1, :]
            arow_ref[:, j * SSM_TILE_W + SSM_HALF + gl * SSM_STATE:
                     j * SSM_TILE_W + SSM_HALF + (gl + 1) * SSM_STATE] = ai[g:g + 1, :]
            br = brt_ref[g]
            bi = bit_ref[g]
            crg = cr[g:g + 1, :]
            cig = ci[g:g + 1, :]
            bbs_ref[ch, cols] = crg * br - cig * bi
            bbs_ref[ch, cols_im] = crg * bi + cig * br
            ccs_ref[cols, ch] = crt_ref[g]
            ccs_ref[cols_im, ch] = -cit_ref[g]
        bb_ref[j] = bbs_ref[...].astype(BF16)
        cc_ref[j] = ccs_ref[...].astype(BF16)


def _ssm_prep(lam_re, lam_im, log_dt, b_re, b_im, c_re, c_im):
    depth = lam_re.shape[0]
    brt = jnp.swapaxes(b_re, 2, 3)
    bit = jnp.swapaxes(b_im, 2, 3)
    crt = jnp.swapaxes(c_re, 2, 3)
    cit = jnp.swapaxes(c_im, 2, 3)

    def spec(*shape):
        nd = len(shape)
        return pl.BlockSpec((None,) + shape, lambda l, _nd=nd: (l,) + (0,) * _nd)

    return pl.pallas_call(
        _ssm_prep_kernel,
        grid=(depth,),
        in_specs=[spec(SSM_GROUPS, SSM_STATE), spec(SSM_GROUPS, SSM_STATE), spec(SSM_GROUPS, 1),
                  spec(SSM_GROUPS, SSM_GROUP_CH, SSM_STATE), spec(SSM_GROUPS, SSM_GROUP_CH, SSM_STATE),
                  spec(SSM_GROUPS, SSM_STATE, SSM_GROUP_CH), spec(SSM_GROUPS, SSM_STATE, SSM_GROUP_CH)],
        out_specs=(spec(1, 2 * SSM_W), spec(SSM_TILES, LANES, SSM_TILE_W), spec(SSM_TILES, SSM_TILE_W, LANES)),
        out_shape=(jax.ShapeDtypeStruct((depth, 1, 2 * SSM_W), F32),
                   jax.ShapeDtypeStruct((depth, SSM_TILES, LANES, SSM_TILE_W), BF16),
                   jax.ShapeDtypeStruct((depth, SSM_TILES, SSM_TILE_W, LANES), BF16)),
        scratch_shapes=[pltpu.VMEM((LANES, SSM_TILE_W), F32), pltpu.VMEM((SSM_TILE_W, LANES), F32)],
        compiler_params=pltpu.CompilerParams(dimension_semantics=("arbitrary",)),
        name="ssm_prep",
    )(lam_re, lam_im, log_dt.reshape(depth, SSM_GROUPS, 1), brt, bit, crt, cit)


def _proj_chunk(xb, w_in_ref, proj_ref, c0):
    assert c0 + PROJ_CT <= G0
    y = _dot(xb, w_in_ref[:, c0:c0 + PROJ_CT])
    if any(z0 <= c0 < z0 + BRANCH_W for z0 in (ZA0, ZB0, ZC0)):
        y = _silu(y)
    elif c0 < K0:
        y = y * ATTN_SCALE
    proj_ref[:, c0:c0 + PROJ_CT] = y


def _s5_output(y_lin, u, ptab_ref, gluw_ref):
    y = y_lin + ptab_ref[ROW_DSKIP_GLUB:ROW_DSKIP_GLUB + 1, 0:BRANCH_W] * u
    y = _gelu_tanh(y)
    glu_b = ptab_ref[ROW_DSKIP_GLUB:ROW_DSKIP_GLUB + 1, BRANCH_W:2 * BRANCH_W]
    return y * _sigmoid(_dot(y.astype(BF16), gluw_ref[...]) + glu_b)


def _gate_chunk(xb, w_in_ref, g0):
    return _sigmoid(_dot(xb, w_in_ref[:, g0:g0 + PROJ_CT]))


def _merge(x, w_in_ref, proj_ref, ybr_ref, merged_ref, wread_ref, wo_ref, ptab_ref, gates_ref=None):
    xb = x.astype(BF16)
    ybs = [(ybr_ref[:, b * BRANCH_W:(b + 1) * BRANCH_W] * proj_ref[:, z0:z0 + BRANCH_W]).astype(BF16)
           for b, z0 in enumerate((ZA0, ZB0, ZC0))]
    for n0 in range(0, D_MODEL, PROJ_CT):
        merged = None
        for b in range(N_BRANCH):
            g0 = b * D_MODEL + n0
            gate = (_gate_chunk(xb, w_in_ref, G0 + g0) if gates_ref is None
                    else gates_ref[:, g0:g0 + PROJ_CT])
            term = gate * _dot(ybs[b], wread_ref[b, :, n0:n0 + PROJ_CT])
            merged = term if merged is None else merged + term
        merged_ref[:, n0:n0 + PROJ_CT] = merged.astype(BF16)
    out = _dot(merged_ref[...], wo_ref[...])
    return _layernorm(ALPHA * x + out, ptab_ref[ROW_LN_G:ROW_LN_G + 1, :], ptab_ref[ROW_LN_B:ROW_LN_B + 1, :])


def _sgu_layernorm(v, ptab_ref):
    return _layernorm(v, ptab_ref[ROW_SGU_LN:ROW_SGU_LN + 1, 0:BRANCH_W],
                      ptab_ref[ROW_SGU_LN:ROW_SGU_LN + 1, BRANCH_W:2 * BRANCH_W])


def _prompt_kernel(layer, sinks_ref, x_ref, w_in_ref, arow_ref, bb_ref, cc_ref, ptab_ref, gluw_ref,
                   sguw_ref, wread_ref, wo_ref,
                   y_ref, kwin_ref, vwin_ref, hre_ref, him_ref,
                   proj_ref, ybr_ref, kvprev_ref, bias_ref, atab_ref, apw_ref, hs_ref, hcarry_ref,
                   uslab_ref, yslab_ref, merged_ref, gates_ref, hsb_ref):
    bi = pl.program_id(0)
    ji = pl.program_id(1)

    @pl.when(jnp.logical_and(bi == 0, ji == 0))
    def _init_tables():
        qa = lax.broadcasted_iota(jnp.int32, (SUB, 2 * SUB), 0)
        kk = lax.broadcasted_iota(jnp.int32, (SUB, 2 * SUB), 1)
        dist = qa + WINDOW - kk
        valid = jnp.logical_and(dist >= 0, dist <= WINDOW)
        distf = dist.astype(F32)
        for h in range(N_HEADS):
            bias_ref[h] = jnp.where(valid, -_slope(h) * distf, NEG_INF)
        for j in range(SSM_TILES):
            a_r = arow_ref[:, _re(j)]
            a_i = arow_ref[:, _im(j)]
            pr, pi = a_r, a_i
            for s in range(SCAN_LEN):
                atab_ref[s, :, _re(j)] = jnp.broadcast_to(pr, (SUBLANES, SSM_HALF))
                atab_ref[s, :, _im(j)] = jnp.broadcast_to(pi, (SUBLANES, SSM_HALF))
                if s + 1 < SCAN_LEN:
                    pr, pi = pr * a_r - pi * a_i, pr * a_i + pi * a_r
            for k in range(2):
                pr, pi = pr * pr - pi * pi, 2.0 * (pr * pi)
                apw_ref[k, :, _re(j)] = jnp.broadcast_to(pr, (SUBLANES, SSM_HALF))
                apw_ref[k, :, _im(j)] = jnp.broadcast_to(pi, (SUBLANES, SSM_HALF))

    @pl.when(ji == 0)
    def _reset_carries():
        kvprev_ref[...] = jnp.zeros((SUB, 2 * LANES), F32)
        hcarry_ref[...] = jnp.zeros((SUBLANES, 2 * SSM_W), F32)

    xb = x_ref[0].astype(BF16)
    late_chunks = [z0 + c for z0 in (ZA0, ZB0, ZC0) for c in range(0, BRANCH_W, PROJ_CT)]

    def project_late(n):
        for _ in range(n):
            if late_chunks:
                _proj_chunk(xb, w_in_ref, proj_ref, late_chunks.pop(0))

    for c0 in range(UB0, UB0 + BRANCH_W, PROJ_CT):
        _proj_chunk(xb, w_in_ref, proj_ref, c0)

    for sb in range(N_SUB):
        for j in range(SSM_TILES):
            uslab_ref[sb * SSM_TILES + j] = proj_ref[sb * SUB:(sb + 1) * SUB, UB0 + j * LANES:UB0 + (j + 1) * LANES]
        for j in range(SSM_TILES):
            u_perm = jnp.concatenate(
                [uslab_ref[sb * SSM_TILES + j, pl.ds(s, SCAN_CHUNKS, stride=SCAN_LEN), :]
                 for s in range(SCAN_LEN)], axis=0)
            hs_ref[sb * SUB:(sb + 1) * SUB, j * SSM_TILE_W:(j + 1) * SSM_TILE_W] = _dot(
                u_perm.astype(BF16), bb_ref[j])

    for c0 in list(range(Q0, ZA0, PROJ_CT)) + list(range(UC0, ZC0, PROJ_CT)):
        _proj_chunk(xb, w_in_ref, proj_ref, c0)
    proj_ref[:, VC0:VC0 + BRANCH_W] = _sgu_layernorm(proj_ref[:, VC0:VC0 + BRANCH_W], ptab_ref)

    srow = lax.broadcasted_iota(jnp.int32, (SUBLANES, SSM_HALF), 0)
    for sb, j in [(sb, j) for sb in range(N_SUB) for j in range(SSM_TILES)]:
        r_base = sb * SUB
        re, im = _re(j), _im(j)
        a_r = atab_ref[0, :, re]
        a_i = atab_ref[0, :, im]
        project_late(1)
        c_r = jnp.broadcast_to(hcarry_ref[SUBLANES - 1:SUBLANES, re], (SUBLANES, SSM_HALF))
        c_i = jnp.broadcast_to(hcarry_ref[SUBLANES - 1:SUBLANES, im], (SUBLANES, SSM_HALF))
        h_r = hs_ref[r_base:r_base + SUBLANES, re] + jnp.where(srow == 0, a_r * c_r - a_i * c_i, 0.0)
        h_i = hs_ref[r_base:r_base + SUBLANES, im] + jnp.where(srow == 0, a_r * c_i + a_i * c_r, 0.0)
        hs_ref[r_base:r_base + SUBLANES, re] = h_r
        hs_ref[r_base:r_base + SUBLANES, im] = h_i
        for s in range(1, SCAN_LEN):
            rows = slice(r_base + s * SUBLANES, r_base + (s + 1) * SUBLANES)
            h_r, h_i = (hs_ref[rows, re] + (a_r * h_r - a_i * h_i),
                        hs_ref[rows, im] + (a_r * h_i + a_i * h_r))
            hs_ref[rows, re] = h_r
            hs_ref[rows, im] = h_i
        e_r, e_i = h_r, h_i
        for k, d in enumerate((1, 2, 4)):
            if k == 0:
                p_r, p_i = atab_ref[SCAN_LEN - 1, :, re], atab_ref[SCAN_LEN - 1, :, im]
            else:
                p_r, p_i = apw_ref[k - 1, :, re], apw_ref[k - 1, :, im]
            s_r = pltpu.roll(e_r, d, 0)
            s_i = pltpu.roll(e_i, d, 0)
            e_r, e_i = (e_r + jnp.where(srow >= d, p_r * s_r - p_i * s_i, 0.0),
                        e_i + jnp.where(srow >= d, p_r * s_i + p_i * s_r, 0.0))
        hcarry_ref[:, re] = e_r
        hcarry_ref[:, im] = e_i
        hre_ref[0, :, j * SSM_HALF:(j + 1) * SSM_HALF] = e_r
        him_ref[0, :, j * SSM_HALF:(j + 1) * SSM_HALF] = e_i
        in_r = jnp.where(srow == 0, 0.0, pltpu.roll(e_r, 1, 0))
        in_i = jnp.where(srow == 0, 0.0, pltpu.roll(e_i, 1, 0))
        in_r2 = jnp.concatenate([in_r, in_r], axis=0)
        in_i2 = jnp.concatenate([in_i, in_i], axis=0)
        for s in range(0, SCAN_LEN, 2):
            rows = slice(r_base + s * SUBLANES, r_base + (s + 2) * SUBLANES)
            t_r = jnp.concatenate([atab_ref[s, :, re], atab_ref[s + 1, :, re]], axis=0)
            t_i = jnp.concatenate([atab_ref[s, :, im], atab_ref[s + 1, :, im]], axis=0)
            hsb_ref[rows, re] = (hs_ref[rows, re] + (t_r * in_r2 - t_i * in_i2)).astype(BF16)
            hsb_ref[rows, im] = (hs_ref[rows, im] + (t_r * in_i2 + t_i * in_r2)).astype(BF16)
    assert not late_chunks
    for sb in range(N_SUB):
        for j in range(SSM_TILES):
            y_perm = _dot(hsb_ref[sb * SUB:(sb + 1) * SUB, j * SSM_TILE_W:(j + 1) * SSM_TILE_W],
                          cc_ref[j])
            for s in range(SCAN_LEN):
                yslab_ref[sb * SSM_TILES + j, pl.ds(s, SCAN_CHUNKS, stride=SCAN_LEN), :] = (
                    y_perm[s * SUBLANES:(s + 1) * SUBLANES, :])
    y_lin = jnp.concatenate(
        [jnp.concatenate([yslab_ref[sb * SSM_TILES + j] for j in range(SSM_TILES)], axis=1)
         for sb in range(N_SUB)], axis=0)
    ybr_ref[:, BRANCH_W:2 * BRANCH_W] = _s5_output(y_lin, proj_ref[:, UB0:UB0 + BRANCH_W], ptab_ref, gluw_ref)

    lane2 = lax.broadcasted_iota(jnp.int32, (2 * SUB, LANES), 1)
    kcol = lax.broadcasted_iota(jnp.int32, (SUB, 2 * SUB), 1)
    lane = lax.broadcasted_iota(jnp.int32, (SUB, LANES), 1)
    lo = lane < HEAD_DIM
    first_neg = jnp.where(ji == 0, NEG_INF, 0.0).astype(F32)
    kv_prev = kvprev_ref[...]
    gate_chunks = list(range(0, N_BRANCH * D_MODEL, PROJ_CT))
    per_tile = [0, 0, 2, 2, 2, 2, 2, 2]

    def project_gates(n):
        for _ in range(n):
            if gate_chunks:
                g0 = gate_chunks.pop(0)
                gates_ref[:, g0:g0 + PROJ_CT] = _gate_chunk(xb, w_in_ref, G0 + g0)

    for sb in range(N_SUB):
        rows = slice(sb * SUB, (sb + 1) * SUB)
        kv_cur = proj_ref[rows, K0:K0 + 2 * LANES]
        k_dup = _dup_halves(jnp.concatenate([kv_prev[:, 0:LANES], kv_cur[:, 0:LANES]], axis=0), lane2 < HEAD_DIM)
        v_dup = _dup_halves(jnp.concatenate([kv_prev[:, LANES:], kv_cur[:, LANES:]], axis=0), lane2 < HEAD_DIM)
        k_dup = [k.astype(BF16) for k in k_dup]
        v_dup = [v.astype(BF16) for v in v_dup]
        for t in range(N_HEADS // 2):
            project_gates(per_tile[sb * (N_HEADS // 2) + t])
            qt = proj_ref[rows, Q0 + t * LANES:Q0 + (t + 1) * LANES]
            kv_head = (2 * t) // GQA_GROUP
            halves = []
            for half in range(2):
                h = 2 * t + half
                qm = jnp.where(lo if half == 0 else jnp.logical_not(lo), qt, 0.0).astype(BF16)
                s = _dot_nt(qm, k_dup[kv_head]) + bias_ref[h]
                if sb == 0:
                    s = s + jnp.where(kcol < SUB, first_neg, 0.0)
                sink = sinks_ref[layer, h]
                m = jnp.maximum(jnp.max(s, axis=-1, keepdims=True), sink)
                p = jnp.exp(s - m)
                denom = jnp.sum(p, axis=-1, keepdims=True) + jnp.exp(sink - m)
                o = _dot(p.astype(BF16), v_dup[kv_head])
                halves.append(o * (1.0 / denom))
            ybr_ref[rows, t * LANES:(t + 1) * LANES] = jnp.where(lo, halves[0], halves[1])
        kv_prev = kv_cur
    kvprev_ref[...] = kv_prev
    kwin_ref[0] = kv_prev[:, 0:LANES]
    vwin_ref[0] = kv_prev[:, LANES:]
    project_gates(len(gate_chunks))

    trow = lax.broadcasted_iota(jnp.int32, (CHUNK, CHUNK), 0)
    tcol = lax.broadcasted_iota(jnp.int32, (CHUNK, CHUNK), 1)
    tril = trow >= tcol
    for g in range(SGU_GROUPS):
        wm = jnp.where(tril, sguw_ref[g], 0.0).astype(BF16)
        b_row = ptab_ref[ROW_SGU_B:ROW_SGU_B + 1, g * CHUNK:(g + 1) * CHUNK]
        b_col = jnp.sum(jnp.where(trow == tcol, b_row, 0.0), axis=1, keepdims=True)
        for sb in range(N_SUB):
            rows = slice(sb * SUB, (sb + 1) * SUB)
            vn_g = proj_ref[rows, VC0 + g * LANES:VC0 + (g + 1) * LANES]
            sg = _dot(wm, vn_g.astype(BF16)) + b_col
            ybr_ref[rows, 2 * BRANCH_W + g * LANES:2 * BRANCH_W + (g + 1) * LANES] = (
                proj_ref[rows, UC0 + g * LANES:UC0 + (g + 1) * LANES] * sg)

    y_ref[0] = _merge(x_ref[0], w_in_ref, proj_ref, ybr_ref, merged_ref, wread_ref, wo_ref, ptab_ref, gates_ref)


def _layer_spec(layer, shape):
    nd = len(shape)
    return pl.BlockSpec((None,) + tuple(shape), lambda *_, _nd=nd: (layer,) + (0,) * _nd,
                        pipeline_mode=pl.Buffered(1))


def _prompt_layer(layer, x, sinks, lw):
    nb, t, _ = x.shape
    grid = (nb, t // TB)
    weights = (lw['w_in'], lw['a_row'], lw['bb'], lw['cc'], lw['ptab'], lw['glu_w'], lw['sgu_w'],
               lw['w_read'], lw['w_o'])
    in_specs = [pl.BlockSpec(memory_space=pltpu.SMEM),
                pl.BlockSpec((1, TB, D_MODEL), lambda b, j: (b, j, 0))]
    in_specs += [_layer_spec(layer, w.shape[1:]) for w in weights]
    out_shape = (jax.ShapeDtypeStruct((nb, t, D_MODEL), F32),
                 jax.ShapeDtypeStruct((nb, SUB, LANES), F32),
                 jax.ShapeDtypeStruct((nb, SUB, LANES), F32),
                 jax.ShapeDtypeStruct((nb, SUBLANES, SSM_W), F32),
                 jax.ShapeDtypeStruct((nb, SUBLANES, SSM_W), F32))
    out_specs = (pl.BlockSpec((1, TB, D_MODEL), lambda b, j: (b, j, 0)),
                 pl.BlockSpec((1, SUB, LANES), lambda b, j: (b, 0, 0)),
                 pl.BlockSpec((1, SUB, LANES), lambda b, j: (b, 0, 0)),
                 pl.BlockSpec((1, SUBLANES, SSM_W), lambda b, j: (b, 0, 0)),
                 pl.BlockSpec((1, SUBLANES, SSM_W), lambda b, j: (b, 0, 0)))
    scratch = [pltpu.VMEM((TB, G0), F32),
               pltpu.VMEM((TB, N_BRANCH * BRANCH_W), F32),
               pltpu.VMEM((SUB, 2 * LANES), F32),
               pltpu.VMEM((N_HEADS, SUB, 2 * SUB), F32),
               pltpu.VMEM((SCAN_LEN, SUBLANES, 2 * SSM_W), F32),
               pltpu.VMEM((2, SUBLANES, 2 * SSM_W), F32),
               pltpu.VMEM((TB, 2 * SSM_W), F32),
               pltpu.VMEM((SUBLANES, 2 * SSM_W), F32),
               pltpu.VMEM((N_SUB * SSM_TILES, SUB, LANES), F32),
               pltpu.VMEM((N_SUB * SSM_TILES, SUB, LANES), F32),
               pltpu.VMEM((TB, D_MODEL), BF16),
               pltpu.VMEM((TB, N_BRANCH * D_MODEL), F32),
               pltpu.VMEM((TB, 2 * SSM_W), BF16)]
    return pl.pallas_call(
        functools.partial(_prompt_kernel, layer),
        grid=grid,
        in_specs=in_specs,
        out_specs=out_specs,
        out_shape=out_shape,
        scratch_shapes=scratch,
        compiler_params=pltpu.CompilerParams(
            dimension_semantics=("arbitrary", "arbitrary"),
            vmem_limit_bytes=VMEM_LIMIT_BYTES),
        name="prompt_layer",
    )(sinks, x, *weights)


def _sample_kernel(layer, sinks_ref, x_ref, ck_ref, cv_ref, h0re_ref, h0im_ref, w_in_ref, arow_ref, bb_ref, cc_ref,
                   ptab_ref, gluw_ref, sguw_ref, wread_ref, wo_ref,
                   y_ref, kout_ref, vout_ref, hre_ref, him_ref, vn_ref,
                   proj_ref, ybr_ref, merged_ref):
    n_seq = x_ref.shape[0]
    win = ck_ref.shape[1]
    xb = x_ref[...].astype(BF16)
    for c0 in range(0, G0, PROJ_CT):
        _proj_chunk(xb, w_in_ref, proj_ref, c0)
    kout_ref[...] = proj_ref[:, K0:K0 + LANES]
    vout_ref[...] = proj_ref[:, V0:V0 + LANES]

    row = lax.broadcasted_iota(jnp.int32, (SUBLANES, LANES), 0)
    lane = lax.broadcasted_iota(jnp.int32, (SUBLANES, LANES), 1)
    lane_lo = lane < HEAD_DIM
    row_even = (row % 2) == 0
    sel = jnp.logical_or(jnp.logical_and(row_even, lane_lo),
                         jnp.logical_and(jnp.logical_not(row_even), jnp.logical_not(lane_lo)))
    swap = (row // GQA_GROUP) != (row % 2)
    slope = jnp.exp2(-(row.astype(F32) + 1.0))
    bias = -slope * (win - lane).astype(F32)
    sink = jnp.zeros((SUBLANES, 1), F32)
    for h in range(N_HEADS):
        sink = jnp.where(row[:, 0:1] == h, sinks_ref[layer, h], sink)

    def seq_group_step(gi, carry):
        r0 = pl.multiple_of(gi * SUBLANES, SUBLANES)
        q_rows = proj_ref[pl.ds(r0, SUBLANES), Q0:Q0 + BRANCH_W]
        k_rows = proj_ref[pl.ds(r0, SUBLANES), K0:K0 + LANES]
        v_rows = proj_ref[pl.ds(r0, SUBLANES), V0:V0 + LANES]
        q8s, scores = [], []
        for r in range(SUBLANES):
            q8 = jnp.zeros((SUBLANES, LANES), F32)
            for t in range(N_HEADS // 2):
                qt = jnp.broadcast_to(q_rows[r:r + 1, t * LANES:(t + 1) * LANES], (SUBLANES, LANES))
                q8 = jnp.where(jnp.logical_and(sel, row // 2 == t), qt, q8)
            q8 = jnp.where(swap, pltpu.roll(q8, HEAD_DIM, 1), q8)
            q8s.append(q8)
            scores.append(_dot_nt(q8.astype(BF16), ck_ref[r0 + r].astype(BF16)) + bias)
        s = jnp.concatenate(scores, axis=0)
        k_own = jnp.concatenate([jnp.broadcast_to(k_rows[r:r + 1, :], (SUBLANES, LANES))
                                 for r in range(SUBLANES)], axis=0)
        s_own = jnp.sum(jnp.concatenate(q8s, axis=0) * k_own, axis=-1, keepdims=True)
        sink_g = jnp.concatenate([sink] * SUBLANES, axis=0)
        m = jnp.maximum(jnp.maximum(jnp.max(s, axis=-1, keepdims=True), s_own), sink_g)
        p = jnp.exp(s - m)
        p_own = jnp.exp(s_own - m)
        inv = 1.0 / (jnp.sum(p, axis=-1, keepdims=True) + p_own + jnp.exp(sink_g - m))
        outs = [[] for _ in range(N_HEADS // 2)]
        for r in range(SUBLANES):
            rows8 = slice(r * SUBLANES, (r + 1) * SUBLANES)
            o = (_dot(p[rows8, :].astype(BF16), cv_ref[r0 + r].astype(BF16))
                 + p_own[rows8, :] * v_rows[r:r + 1, :]) * inv[rows8, :]
            o = jnp.where(swap, pltpu.roll(o, HEAD_DIM, 1), o)
            o = jnp.where(sel, o, 0.0)
            for t in range(N_HEADS // 2):
                outs[t].append(o[2 * t:2 * t + 1, :] + o[2 * t + 1:2 * t + 2, :])
        for t in range(N_HEADS // 2):
            ybr_ref[pl.ds(r0, SUBLANES), t * LANES:(t + 1) * LANES] = jnp.concatenate(outs[t], axis=0)
        return carry

    lax.fori_loop(0, n_seq // SUBLANES, seq_group_step, 0, unroll=2)

    vn = _sgu_layernorm(proj_ref[:, VC0:VC0 + BRANCH_W], ptab_ref)
    vn_ref[...] = vn
    for g in range(SGU_GROUPS):
        w00 = sguw_ref[g, 0:1, 0:1]
        b0 = ptab_ref[ROW_SGU_B:ROW_SGU_B + 1, g * CHUNK:g * CHUNK + 1]
        cols = slice(g * LANES, (g + 1) * LANES)
        ybr_ref[:, 2 * BRANCH_W + g * LANES:2 * BRANCH_W + (g + 1) * LANES] = (
            proj_ref[:, UC0 + g * LANES:UC0 + (g + 1) * LANES] * (w00 * vn[:, cols] + b0))

    u = proj_ref[:, UB0:UB0 + BRANCH_W]
    y_tiles = []
    for j in range(SSM_TILES):
        re, im = _re(j), _im(j)
        xs = _dot(u[:, j * LANES:(j + 1) * LANES].astype(BF16), bb_ref[j])
        a_r = arow_ref[:, re]
        a_i = arow_ref[:, im]
        natural = slice(j * SSM_HALF, (j + 1) * SSM_HALF)
        h0r = h0re_ref[:, natural]
        h0i = h0im_ref[:, natural]
        hr = xs[:, 0:SSM_HALF] + (a_r * h0r - a_i * h0i)
        hi = xs[:, SSM_HALF:] + (a_r * h0i + a_i * h0r)
        hre_ref[:, natural] = hr
        him_ref[:, natural] = hi
        y_tiles.append(_dot(jnp.concatenate([hr, hi], axis=1).astype(BF16), cc_ref[j]))
    y_lin = jnp.concatenate(y_tiles, axis=1)
    ybr_ref[:, BRANCH_W:2 * BRANCH_W] = _s5_output(y_lin, u, ptab_ref, gluw_ref)

    y_ref[...] = _merge(x_ref[...], w_in_ref, proj_ref, ybr_ref, merged_ref, wread_ref, wo_ref, ptab_ref)


def _sample_layer(layer, x, sinks, ck, cv, h0_re, h0_im, lw):
    n = x.shape[0]
    per_layer = (ck, cv, h0_re, h0_im, lw['w_in'], lw['a_row'], lw['bb'], lw['cc'], lw['ptab'], lw['glu_w'],
                 lw['sgu_w'], lw['w_read'], lw['w_o'])
    in_specs = [pl.BlockSpec(memory_space=pltpu.SMEM),
                pl.BlockSpec(x.shape, lambda i: (0, 0), pipeline_mode=pl.Buffered(1))]
    in_specs += [_layer_spec(layer, a.shape[1:]) for a in per_layer]
    out_shape = (jax.ShapeDtypeStruct((n, D_MODEL), F32),
                 jax.ShapeDtypeStruct((n, LANES), F32),
                 jax.ShapeDtypeStruct((n, LANES), F32),
                 jax.ShapeDtypeStruct((n, SSM_W), F32),
                 jax.ShapeDtypeStruct((n, SSM_W), F32),
                 jax.ShapeDtypeStruct((n, BRANCH_W), F32))
    out_specs = tuple(pl.BlockSpec(s.shape, lambda i: (0, 0)) for s in out_shape)
    scratch = [pltpu.VMEM((n, G0), F32), pltpu.VMEM((n, N_BRANCH * BRANCH_W), F32),
               pltpu.VMEM((n, D_MODEL), BF16)]
    return pl.pallas_call(
        functools.partial(_sample_kernel, layer),
        grid=(1,),
        in_specs=in_specs,
        out_specs=out_specs,
        out_shape=out_shape,
        scratch_shapes=scratch,
        compiler_params=pltpu.CompilerParams(
            dimension_semantics=("arbitrary",), vmem_limit_bytes=VMEM_LIMIT_BYTES),
        name="sample_layer",
    )(sinks, x, *per_layer)


def _param_table(ssm_d, glu_b, sgu_ln_g, sgu_ln_b, ln_g, ln_b, sgu_b):
    depth = ln_g.shape[0]
    rows = [jnp.concatenate([ssm_d, glu_b], axis=1),
            jnp.concatenate([sgu_ln_g, sgu_ln_b], axis=1),
            ln_g, ln_b,
            jnp.concatenate([sgu_b.reshape(depth, BRANCH_W), jnp.zeros((depth, BRANCH_W), F32)], axis=1)]
    rows += [jnp.zeros((depth, D_MODEL), F32)] * (PARAM_ROWS - len(rows))
    return jnp.stack(rows, axis=1)


def kernel(x_prompt, x_sample, cache_k_win, cache_v_win, state_ssm_re, state_ssm_im, w_in, attn_sinks,
           ssm_lambda_re, ssm_lambda_im, ssm_log_dt, ssm_b_re, ssm_b_im, ssm_c_re, ssm_c_im, ssm_d,
           glu_w, glu_b, sgu_ln_g, sgu_ln_b, sgu_w, sgu_b, w_read, w_o, ln_g, ln_b):
    nb, t, _ = x_prompt.shape
    ns = x_sample.shape[0]
    depth, _, win = cache_k_win.shape[:3]
    assert t % TB == 0 and win == WINDOW == SUB and ns % SUBLANES == 0

    a_row, bb, cc = _ssm_prep(ssm_lambda_re, ssm_lambda_im, ssm_log_dt, ssm_b_re, ssm_b_im, ssm_c_re, ssm_c_im)
    lw = dict(w_in=w_in.astype(BF16), a_row=a_row, bb=bb, cc=cc,
              ptab=_param_table(ssm_d, glu_b, sgu_ln_g, sgu_ln_b, ln_g, ln_b, sgu_b),
              glu_w=glu_w.astype(BF16), sgu_w=sgu_w, w_read=w_read.astype(BF16), w_o=w_o.astype(BF16))
    ck = cache_k_win.reshape(depth, ns, win, KV_HEADS * HEAD_DIM)
    cv = cache_v_win.reshape(depth, ns, win, KV_HEADS * HEAD_DIM)
    h0_re = state_ssm_re.reshape(depth, ns, SSM_W)
    h0_im = state_ssm_im.reshape(depth, ns, SSM_W)

    xp = x_prompt
    xs = x_sample.reshape(ns, D_MODEL)
    kp, vp, hrp, hip = [], [], [], []
    ksm, vsm, hrs, his, vcs = [], [], [], [], []
    for l in range(depth):
        xp, kwin, vwin, hre, him = _prompt_layer(l, xp, attn_sinks, lw)
        kp.append(kwin.reshape(nb, SUB, KV_HEADS, HEAD_DIM))
        vp.append(vwin.reshape(nb, SUB, KV_HEADS, HEAD_DIM))
        hrp.append(hre[:, SUBLANES - 1, :].reshape(nb, SSM_GROUPS, SSM_STATE))
        hip.append(him[:, SUBLANES - 1, :].reshape(nb, SSM_GROUPS, SSM_STATE))

        xs, k_s, v_s, hre_s, him_s, vn_s = _sample_layer(l, xs, attn_sinks, ck, cv, h0_re, h0_im, lw)
        ksm.append(k_s.reshape(ns, 1, KV_HEADS, HEAD_DIM))
        vsm.append(v_s.reshape(ns, 1, KV_HEADS, HEAD_DIM))
        hrs.append(hre_s.reshape(ns, SSM_GROUPS, SSM_STATE))
        his.append(him_s.reshape(ns, SSM_GROUPS, SSM_STATE))
        vcs.append(vn_s.reshape(ns, 1, BRANCH_W))
    return (xp, xs.reshape(ns, 1, D_MODEL), jnp.stack(kp), jnp.stack(vp), jnp.stack(hrp), jnp.stack(hip),
            jnp.stack(ksm), jnp.stack(vsm), jnp.stack(hrs), jnp.stack(his), jnp.stack(vcs))
```
